```python
import jax, jax.numpy as jnp
from jax import lax
import numpy as np

D_MODEL = 1024
BATCH = 8
SEQ = 2048
DEPTH = 4
DEC_BATCH = 128
DEC_SEQ = 4
PAST_LEN = 16384
PAGE_SIZE = 128

N_MIXERS = 2
N_A_LAYERS = (DEPTH + 1) // 2
N_B_LAYERS = DEPTH // 2
CONV_W = 4
EPS = 1e-6
CHUNK = 64
M_INIT = -1e30
A_HEADS = 4
A_DQK = 256
A_DV = 512
A_QK = A_HEADS * A_DQK
A_VD = A_HEADS * A_DV
A_IN = 2 * A_QK + 2 * A_VD + 2 * A_HEADS
B_INNER = 2 * D_MODEL
B_HEADDIM = 64
B_HEADS = B_INNER // B_HEADDIM
B_GROUPS = 4
B_HPG = B_HEADS // B_GROUPS
B_STATE = 128
B_CONV_CH = B_INNER + 2 * B_GROUPS * B_STATE
B_IN = B_INNER + B_CONV_CH + B_HEADS
MEM_LEN = 256
X_HEADS = 4
X_HD = D_MODEL // X_HEADS
D_FF = 2816
N_EXPERTS = 8
TOP_K = 2
D_FF_E = 3584

kernel_name = "hybrid_mlstm_ssd_memxattn_moe_step"


def _rms(x):
    xf = x.astype(jnp.float32)
    return xf * lax.rsqrt(jnp.mean(xf * xf, axis=-1, keepdims=True) + EPS)


def rmsnorm(x, g):
    return (_rms(x) * g.astype(jnp.float32)).astype(x.dtype)


def causal_conv(x, buf, w, b):
    T = x.shape[1]
    xp = jnp.concatenate([buf.astype(x.dtype), x], axis=1)
    y = b + w[CONV_W - 1] * x
    for j in range(CONV_W - 1):
        y = y + w[j] * xp[:, j:j + T]
    return y, xp[:, T:]


def chunked_scan(step, carry, seqs, T):
    L = CHUNK if T % CHUNK == 0 else T
    nc = T // L

    def to_chunks(a):
        return jnp.moveaxis(a.reshape((a.shape[0], nc, L) + a.shape[2:]), 1, 0)

    carry, out = lax.scan(step, carry, tuple(to_chunks(a) for a in seqs))
    out = jnp.moveaxis(out, 0, 1)
    return carry, out.reshape((out.shape[0], T) + out.shape[3:])


def mlstm_chunk(carry, inp):
    C, n, m = carry
    q, k, v, itil, logf = inp
    L = q.shape[1]
    cum = jnp.swapaxes(jnp.cumsum(logf, axis=1), 1, 2)
    ig = jnp.swapaxes(itil, 1, 2)
    causal = jnp.tril(jnp.ones((L, L), dtype=bool))
    log_intra = jnp.where(causal, cum[..., :, None] - cum[..., None, :] + ig[..., None, :], -jnp.inf)
    log_inter = cum + m[..., None]
    m_t = jnp.maximum(log_inter, jnp.max(log_intra, axis=-1))
    w_intra = jnp.exp(log_intra - m_t[..., None])
    w_inter = jnp.exp(log_inter - m_t)
    s = jnp.einsum('bthd,bshd->bhts', q, k) * w_intra
    num = jnp.einsum('bhts,bshv->bthv', s, v) + jnp.einsum('bhvd,bthd->bthv', C, q) * jnp.swapaxes(w_inter, 1, 2)[..., None]
    den = jnp.sum(s, axis=-1) + w_inter * jnp.einsum('bhd,bthd->bht', n, q)
    den = jnp.maximum(jnp.abs(den), jnp.exp(-m_t))
    h = num / jnp.swapaxes(den, 1, 2)[..., None]
    w_last = w_intra[:, :, -1, :]
    decay = w_inter[:, :, -1]
    C_new = decay[..., None, None] * C + jnp.einsum('bhs,bshv,bshd->bhvd', w_last, v, k)
    n_new = decay[..., None] * n + jnp.einsum('bhs,bshd->bhd', w_last, k)
    return (C_new, n_new, m_t[:, :, -1]), h


def mlstm_mixer(u, C0, n0, m0, buf, w_in, conv_w, conv_b, b_i, b_f, g_out, w_out):
    Bsz, T, _ = u.shape
    proj = u @ w_in
    qk_raw = proj[..., :2 * A_QK]
    v = proj[..., 2 * A_QK:2 * A_QK + A_VD]
    o = proj[..., 2 * A_QK + A_VD:2 * A_QK + 2 * A_VD]
    gi = proj[..., 2 * A_QK + 2 * A_VD:2 * A_QK + 2 * A_VD + A_HEADS]
    gf = proj[..., 2 * A_QK + 2 * A_VD + A_HEADS:]
    qk, new_buf = causal_conv(qk_raw, buf, conv_w, conv_b)
    qk = jax.nn.silu(qk).astype(jnp.float32)
    q = qk[..., :A_QK].reshape(Bsz, T, A_HEADS, A_DQK)
    k = qk[..., A_QK:].reshape(Bsz, T, A_HEADS, A_DQK) * (A_DQK ** -0.5)
    vh = v.astype(jnp.float32).reshape(Bsz, T, A_HEADS, A_DV)
    itil = (gi + b_i).astype(jnp.float32)
    logf = jax.nn.log_sigmoid((gf + b_f).astype(jnp.float32))
    carry0 = (C0.astype(jnp.float32), n0.astype(jnp.float32), m0.astype(jnp.float32))
    (C, n, m), h = chunked_scan(mlstm_chunk, carry0, (q, k, vh, itil, logf), T)
    h = jax.nn.sigmoid(o.astype(jnp.float32)).reshape(Bsz, T, A_HEADS, A_DV) * h
    h = (_rms(h).reshape(Bsz, T, A_VD) * g_out.astype(jnp.float32)).astype(u.dtype)
    return h @ w_out, C, n, m, new_buf


def ssd_chunk(h, inp):
    x, dt, a, Bm, Cm = inp
    L = x.shape[1]
    cum = jnp.moveaxis(jnp.cumsum(a, axis=1), 1, -1)
    causal = jnp.tril(jnp.ones((L, L), dtype=bool))
    seg = jnp.exp(jnp.where(causal, cum[..., :, None] - cum[..., None, :], -jnp.inf))
    cb = jnp.einsum('btgn,bsgn->bgts', Cm, Bm)
    dts = jnp.moveaxis(dt, 1, -1)
    att = cb[:, :, None] * seg * dts[..., None, :]
    y = jnp.einsum('bgrts,bsgrp->btgrp', att, x) + jnp.einsum('btgn,bgrpn,bgrt->btgrp', Cm, h, jnp.exp(cum))
    w_last = seg[..., -1, :] * dts
    h_new = jnp.exp(cum[..., -1])[..., None, None] * h + jnp.einsum('bgrs,bsgn,bsgrp->bgrpn', w_last, Bm, x)
    return h_new, y


def ssd_mixer(u, h0, buf, w_in, conv_w, conv_b, dt_bias, A_log, Dskip, g_norm, w_out):
    Bsz, T, _ = u.shape
    proj = u @ w_in
    z = proj[..., :B_INNER]
    xbc_raw = proj[..., B_INNER:B_INNER + B_CONV_CH]
    dt_raw = proj[..., B_INNER + B_CONV_CH:]
    xbc, new_buf = causal_conv(xbc_raw, buf, conv_w, conv_b)
    xbc = jax.nn.silu(xbc).astype(jnp.float32)
    xs = xbc[..., :B_INNER].reshape(Bsz, T, B_GROUPS, B_HPG, B_HEADDIM)
    Bm = xbc[..., B_INNER:B_INNER + B_GROUPS * B_STATE].reshape(Bsz, T, B_GROUPS, B_STATE)
    Cm = xbc[..., B_INNER + B_GROUPS * B_STATE:].reshape(Bsz, T, B_GROUPS, B_STATE)
    dt = jax.nn.softplus(dt_raw.astype(jnp.float32) + dt_bias.astype(jnp.float32)).reshape(Bsz, T, B_GROUPS, B_HPG)
    A = -jnp.exp(A_log.astype(jnp.float32)).reshape(B_GROUPS, B_HPG)
    h0r = h0.astype(jnp.float32).reshape(Bsz, B_GROUPS, B_HPG, B_HEADDIM, B_STATE)
    h, y = chunked_scan(ssd_chunk, h0r, (xs, dt, dt * A, Bm, Cm), T)
    y = y + Dskip.astype(jnp.float32).reshape(B_GROUPS, B_HPG)[..., None] * xs
    yz = (y.reshape(Bsz, T, B_INNER) * jax.nn.silu(z.astype(jnp.float32))).reshape(Bsz, T, B_GROUPS, B_INNER // B_GROUPS)
    yz = (_rms(yz).reshape(Bsz, T, B_INNER) * g_norm.astype(jnp.float32)).astype(u.dtype)
    return yz @ w_out, h.reshape(Bsz, B_HEADS, B_HEADDIM, B_STATE), new_buf


def mem_kv(mem, g, wk, wv):
    mn = rmsnorm(mem, g)
    Bsz = mem.shape[0]
    return (mn @ wk).reshape(Bsz, -1, X_HEADS, X_HD), (mn @ wv).reshape(Bsz, -1, X_HEADS, X_HD)


def cross_attn(u, mk, mv, wq, wo):
    Bsz, T, _ = u.shape
    q = (u @ wq).reshape(Bsz, T, X_HEADS, X_HD)
    s = jnp.einsum('bthd,bmhd->bhtm', q, mk.astype(u.dtype)).astype(jnp.float32) * (X_HD ** -0.5)
    p = jax.nn.softmax(s, axis=-1).astype(u.dtype)
    o = jnp.einsum('bhtm,bmhd->bthd', p, mv.astype(u.dtype))
    return o.reshape(Bsz, T, X_HEADS * X_HD) @ wo


def swiglu(u, wg, wu, wd):
    return (jax.nn.silu(u @ wg) * (u @ wu)) @ wd


def moe_swiglu(u, router, wg, wu, wd):
    logits = (u @ router).astype(jnp.float32)
    top_val, top_idx = lax.top_k(logits, TOP_K)
    top_w = jax.nn.softmax(top_val, axis=-1)
    gate = jnp.einsum('btk,btke->bte', top_w, jax.nn.one_hot(top_idx, N_EXPERTS, dtype=jnp.float32)).astype(u.dtype)
    out = jnp.zeros_like(u)
    for e in range(N_EXPERTS):
        out = out + gate[..., e:e + 1] * swiglu(u, wg[e], wu[e], wd[e])
    return out


def layer_stack(x, mem_k, mem_v, a_C, a_n, a_m, a_conv, b_h, b_conv, p):
    nC, nn_, nm, nac, nh, nbc = [], [], [], [], [], []
    for i in range(DEPTH):
        j = i // N_MIXERS
        u = rmsnorm(x, p['norm_mix'][i])
        if i % N_MIXERS == 0:
            y, C, n, m, cb = mlstm_mixer(u, a_C[j], a_n[j], a_m[j], a_conv[j], p['a_w_in'][j], p['a_conv_w'][j],
                                         p['a_conv_b'][j], p['a_b_i'][j], p['a_b_f'][j], p['a_norm'][j], p['a_w_out'][j])
            nC.append(C); nn_.append(n); nm.append(m); nac.append(cb)
        else:
            y, h, cb = ssd_mixer(u, b_h[j], b_conv[j], p['b_w_in'][j], p['b_conv_w'][j], p['b_conv_b'][j],
                                 p['b_dt_bias'][j], p['b_A_log'][j], p['b_D'][j], p['b_norm'][j], p['b_w_out'][j])
            nh.append(h); nbc.append(cb)
        x = x + y
        u = rmsnorm(x, p['norm_xattn'][i])
        x = x + cross_attn(u, mem_k[i], mem_v[i], p['x_wq'][i], p['x_wo'][i])
        u = rmsnorm(x, p['norm_ffn'][i])
        if i % 2 == 0:
            f = swiglu(u, p['f_w_gate'][i // 2], p['f_w_up'][i // 2], p['f_w_down'][i // 2])
        else:
            f = moe_swiglu(u, p['e_router'][i // 2], p['e_w_gate'][i // 2], p['e_w_up'][i // 2], p['e_w_down'][i // 2])
        x = x + f
    y = rmsnorm(x, p['norm_final'])
    return y, jnp.stack(nC), jnp.stack(nn_), jnp.stack(nm), jnp.stack(nac), jnp.stack(nh), jnp.stack(nbc)


def _w(k, shape, fan_in):
    return jax.random.normal(k, shape, jnp.float32) * (fan_in ** -0.5)


def _gain(k, shape):
    return 1.0 + 0.02 * jax.random.normal(k, shape, jnp.float32)


def setup_inputs(seed: int = 0) -> dict:
    key = jax.random.key(seed)
    ks = list(jax.random.split(key, 48))
    nrm = lambda k, s, sc=1.0: sc * jax.random.normal(k, s, jnp.float32)
    NA, NB = N_A_LAYERS, N_B_LAYERS
    NDn, NMo = (DEPTH + 1) // 2, DEPTH // 2
    dt0 = jnp.exp(jax.random.uniform(ks[30], (NB, B_HEADS), jnp.float32, np.log(1e-3), np.log(1e-1)))
    return {
        'x_prompt': nrm(ks[0], (BATCH, SEQ, D_MODEL)),
        'x_sample': nrm(ks[1], (DEC_BATCH, DEC_SEQ, D_MODEL)),
        'mem_prompt': nrm(ks[2], (BATCH, MEM_LEN, D_MODEL)),
        'state_mlstm_C': nrm(ks[3], (NA, DEC_BATCH, A_HEADS, A_DV, A_DQK), 0.1),
        'state_mlstm_n': nrm(ks[4], (NA, DEC_BATCH, A_HEADS, A_DQK), 0.1),
        'state_mlstm_m': nrm(ks[5], (NA, DEC_BATCH, A_HEADS)),
        'state_mlstm_conv': nrm(ks[6], (NA, DEC_BATCH, CONV_W - 1, 2 * A_QK)),
        'state_ssd_h': nrm(ks[7], (NB, DEC_BATCH, B_HEADS, B_HEADDIM, B_STATE), 0.1),
        'state_ssd_conv': nrm(ks[8], (NB, DEC_BATCH, CONV_W - 1, B_CONV_CH)),
        'cache_mem_k': nrm(ks[9], (DEPTH, DEC_BATCH, MEM_LEN, X_HEADS, X_HD)),
        'cache_mem_v': nrm(ks[10], (DEPTH, DEC_BATCH, MEM_LEN, X_HEADS, X_HD)),
        'norm_mix': _gain(ks[11], (DEPTH, D_MODEL)),
        'norm_xattn': _gain(ks[12], (DEPTH, D_MODEL)),
        'norm_mem': _gain(ks[13], (DEPTH, D_MODEL)),
        'norm_ffn': _gain(ks[14], (DEPTH, D_MODEL)),
        'norm_final': _gain(ks[15], (D_MODEL,)),
        'a_w_in': _w(ks[16], (NA, D_MODEL, A_IN), D_MODEL),
        'a_conv_w': nrm(ks[17], (NA, CONV_W, 2 * A_QK), 0.5),
        'a_conv_b': nrm(ks[18], (NA, 2 * A_QK), 0.02),
        'a_b_i': nrm(ks[19], (NA, A_HEADS), 0.1),
        'a_b_f': jnp.linspace(3.0, 6.0, A_HEADS, dtype=jnp.float32)[None] + nrm(ks[20], (NA, A_HEADS), 0.1),
        'a_norm': _gain(ks[21], (NA, A_VD)),
        'a_w_out': _w(ks[22], (NA, A_VD, D_MODEL), A_VD),
        'b_w_in': _w(ks[23], (NB, D_MODEL, B_IN), D_MODEL),
        'b_conv_w': nrm(ks[24], (NB, CONV_W, B_CONV_CH), 0.5),
        'b_conv_b': nrm(ks[25], (NB, B_CONV_CH), 0.02),
        'b_dt_bias': dt0 + jnp.log(-jnp.expm1(-dt0)),
        'b_A_log': jnp.log(jax.random.uniform(ks[26], (NB, B_HEADS), jnp.float32, 1.0, 16.0)),
        'b_D': _gain(ks[27], (NB, B_HEADS)),
        'b_norm': _gain(ks[28], (NB, B_INNER)),
        'b_w_out': _w(ks[29], (NB, B_INNER, D_MODEL), B_INNER),
        'x_wq': _w(ks[31], (DEPTH, D_MODEL, X_HEADS * X_HD), D_MODEL),
        'x_wk': _w(ks[32], (DEPTH, D_MODEL, X_HEADS * X_HD), D_MODEL),
        'x_wv': _w(ks[33], (DEPTH, D_MODEL, X_HEADS * X_HD), D_MODEL),
        'x_wo': _w(ks[34], (DEPTH, X_HEADS * X_HD, D_MODEL), X_HEADS * X_HD),
        'f_w_gate': _w(ks[35], (NDn, D_MODEL, D_FF), D_MODEL),
        'f_w_up': _w(ks[36], (NDn, D_MODEL, D_FF), D_MODEL),
        'f_w_down': _w(ks[37], (NDn, D_FF, D_MODEL), D_FF),
        'e_router': _w(ks[38], (NMo, D_MODEL, N_EXPERTS), D_MODEL),
        'e_w_gate': _w(ks[39], (NMo, N_EXPERTS, D_MODEL, D_FF_E), D_MODEL),
        'e_w_up': _w(ks[40], (NMo, N_EXPERTS, D_MODEL, D_FF_E), D_MODEL),
        'e_w_down': _w(ks[41], (NMo, N_EXPERTS, D_FF_E, D_MODEL), D_FF_E),
    }


def reference(x_prompt, x_sample, mem_prompt, state_mlstm_C, state_mlstm_n, state_mlstm_m, state_mlstm_conv,
              state_ssd_h, state_ssd_conv, cache_mem_k, cache_mem_v, norm_mix, norm_xattn, norm_mem, norm_ffn,
              norm_final, a_w_in, a_conv_w, a_conv_b, a_b_i, a_b_f, a_norm, a_w_out, b_w_in, b_conv_w, b_conv_b,
              b_dt_bias, b_A_log, b_D, b_norm, b_w_out, x_wq, x_wk, x_wv, x_wo, f_w_gate, f_w_up, f_w_down,
              e_router, e_w_gate, e_w_up, e_w_down):
    p = {'norm_mix': norm_mix, 'norm_xattn': norm_xattn, 'norm_ffn': norm_ffn, 'norm_final': norm_final,
         'a_w_in': a_w_in, 'a_conv_w': a_conv_w, 'a_conv_b': a_conv_b, 'a_b_i': a_b_i, 'a_b_f': a_b_f,
         'a_norm': a_norm, 'a_w_out': a_w_out, 'b_w_in': b_w_in, 'b_conv_w': b_conv_w, 'b_conv_b': b_conv_b,
         'b_dt_bias': b_dt_bias, 'b_A_log': b_A_log, 'b_D': b_D, 'b_norm': b_norm, 'b_w_out': b_w_out,
         'x_wq': x_wq, 'x_wo': x_wo, 'f_w_gate': f_w_gate, 'f_w_up': f_w_up, 'f_w_down': f_w_down,
         'e_router': e_router, 'e_w_gate': e_w_gate, 'e_w_up': e_w_up, 'e_w_down': e_w_down}
    Bp = x_prompt.shape[0]
    kvs = [mem_kv(mem_prompt, norm_mem[i], x_wk[i], x_wv[i]) for i in range(DEPTH)]
    p_mem_k = jnp.stack([kv[0] for kv in kvs])
    p_mem_v = jnp.stack([kv[1] for kv in kvs])
    zC = jnp.zeros((N_A_LAYERS, Bp, A_HEADS, A_DV, A_DQK), jnp.float32)
    zn = jnp.zeros((N_A_LAYERS, Bp, A_HEADS, A_DQK), jnp.float32)
    zm = jnp.full((N_A_LAYERS, Bp, A_HEADS), M_INIT, jnp.float32)
    zac = jnp.zeros((N_A_LAYERS, Bp, CONV_W - 1, 2 * A_QK), x_prompt.dtype)
    zh = jnp.zeros((N_B_LAYERS, Bp, B_HEADS, B_HEADDIM, B_STATE), jnp.float32)
    zbc = jnp.zeros((N_B_LAYERS, Bp, CONV_W - 1, B_CONV_CH), x_prompt.dtype)
    y_prompt, p_C, p_n, p_m, p_aconv, p_h, p_bconv = layer_stack(x_prompt, p_mem_k, p_mem_v, zC, zn, zm, zac, zh, zbc, p)
    y_sample, s_C, s_n, s_m, s_aconv, s_h, s_bconv = layer_stack(
        x_sample, cache_mem_k, cache_mem_v, state_mlstm_C, state_mlstm_n, state_mlstm_m, state_mlstm_conv,
        state_ssd_h, state_ssd_conv, p)
    return (y_prompt, y_sample, p_C, p_n, p_m, p_aconv, p_h, p_bconv, p_mem_k, p_mem_v,
            s_C, s_n, s_m, s_aconv, s_h, s_bconv)
```

```python
import functools

import jax
import jax.numpy as jnp
from jax import lax
from jax.experimental import pallas as pl
from jax.experimental.pallas import tpu as pltpu

F32 = jnp.float32
BF16 = jnp.bfloat16

D_MODEL = 1024
DEPTH = 4
CONV_W = 4
EPS = 1e-6
M_INIT = -1e30
A_HEADS = 4
A_DQK = 256
A_DV = 512
A_QK = A_HEADS * A_DQK
A_VD = A_HEADS * A_DV
B_INNER = 2 * D_MODEL
B_HEADDIM = 64
B_HEADS = B_INNER // B_HEADDIM
B_GROUPS = 4
B_HPG = B_HEADS // B_GROUPS
B_STATE = 128
B_GW = B_HPG * B_HEADDIM
X_HEADS = 4
X_HD = D_MODEL // X_HEADS
N_EXPERTS = 8
D_FF_E = 3584

LANES = 128
SUBLANES = 8
VMEM_LIMIT_BYTES = 56 * 1024 * 1024

A_PROJ_W = 2 * A_QK + 2 * A_VD + 256
B_PROJ_W = 2 * B_INNER + 2 * B_GROUPS * B_STATE + 256
A_GATE_BLK = (2 * A_QK + 2 * A_VD) // LANES
B_DT_BLK = (2 * B_INNER + 2 * B_GROUPS * B_STATE) // LANES

SAMPLE_PAD_T = 8
MOE_TM = 512
MOE_TF = 1792
SCAN_L = 128
PRECISE_LAYERS = 2
PRECISE_TM = 512


def _cparams(sem):
    return pltpu.CompilerParams(dimension_semantics=sem, vmem_limit_bytes=VMEM_LIMIT_BYTES)


def _silu(x):
    return x * jax.nn.sigmoid(x)


def _softplus(x):
    return jnp.maximum(x, 0.0) + jnp.log1p(jnp.exp(-jnp.abs(x)))


def _rms_rows(x):
    return x * lax.rsqrt(jnp.mean(x * x, axis=-1, keepdims=True) + EPS)


def _hl(x):
    hi = x.astype(BF16)
    return hi, (x - hi.astype(F32)).astype(BF16)


def _dotx(a, b, dims, precise):
    d = lambda u, v: lax.dot_general(u, v, dims, preferred_element_type=F32)
    if not precise:
        return d(a.astype(BF16), b.astype(BF16))
    ah, al = _hl(a.astype(F32))
    bh, bl = _hl(b.astype(F32))
    return d(ah, bh) + (d(al, bh) + d(ah, bl))


def _dot_w(xh, xl, w_ref, wl_ref):
    acc = jnp.dot(xh, w_ref[...], preferred_element_type=F32)
    if wl_ref is not None:
        acc = acc + (jnp.dot(xl, w_ref[...], preferred_element_type=F32)
                     + jnp.dot(xh, wl_ref[...], preferred_element_type=F32))
    return acc


def _mm_kernel(*refs, has_gain, has_res, stage, precise):
    it = iter(refs)
    x_ref = next(it)
    w_ref = next(it)
    wl_ref = next(it) if precise else None
    g_ref = next(it) if has_gain else None
    r_ref = next(it) if has_res else None
    o_ref = next(it)
    xs_ref = next(it) if stage else None
    xl_ref = next(it) if precise else None

    if stage:
        @pl.when(pl.program_id(1) == 0)
        def _():
            x = x_ref[...].astype(F32)
            if has_gain:
                x = _rms_rows(x) * g_ref[...]
            if precise:
                xs_ref[...], xl_ref[...] = _hl(x)
            else:
                xs_ref[...] = x.astype(BF16)
        lhs = xs_ref[...]
    else:
        lhs = x_ref[...]
    acc = _dot_w(lhs, xl_ref[...] if precise else None, w_ref, wl_ref)
    if has_res:
        acc = acc + r_ref[...]
    o_ref[...] = acc.astype(o_ref.dtype)


def _mm(x, w, *, w_lo=None, gain=None, res=None, tm, tn, out_dtype=F32, name="mm"):
    M, K = x.shape
    N = w.shape[1]
    tm = min(tm, M)
    assert M % tm == 0 and N % tn == 0, (M, tm, N, tn)
    precise = w_lo is not None
    stage = precise or gain is not None or x.dtype != BF16
    wspec = pl.BlockSpec((K, tn), lambda i, j: (0, j))
    ins = [x, w]
    specs = [pl.BlockSpec((tm, K), lambda i, j: (i, 0)), wspec]
    if precise:
        ins.append(w_lo)
        specs.append(wspec)
    if gain is not None:
        ins.append(gain.reshape(1, K).astype(F32))
        specs.append(pl.BlockSpec((1, K), lambda i, j: (0, 0)))
    if res is not None:
        ins.append(res)
        specs.append(pl.BlockSpec((tm, tn), lambda i, j: (i, j)))
    scratch = [pltpu.VMEM((tm, K), BF16)] * (2 if precise else 1) if stage else []
    return pl.pallas_call(
        functools.partial(_mm_kernel, has_gain=gain is not None, has_res=res is not None, stage=stage,
                          precise=precise),
        grid=(M // tm, N // tn),
        in_specs=specs,
        out_specs=pl.BlockSpec((tm, tn), lambda i, j: (i, j)),
        out_shape=jax.ShapeDtypeStruct((M, N), out_dtype),
        scratch_shapes=scratch,
        compiler_params=_cparams(("parallel", "arbitrary")),
        name=name,
    )(*ins)


def _split_bf16(w):
    bits = lax.bitcast_convert_type(w, jnp.uint32)
    hi = lax.bitcast_convert_type(bits & jnp.uint32(0xFFFF0000), F32)
    return hi.astype(BF16), (w - hi).astype(BF16)


def _norm_kernel(x_ref, g_ref, o_ref):
    o_ref[...] = _rms_rows(x_ref[...]) * g_ref[...]


def _final_norm(x, g, tm):
    M, K = x.shape
    tm = min(tm, M)
    return pl.pallas_call(
        _norm_kernel,
        grid=(M // tm,),
        in_specs=[pl.BlockSpec((tm, K), lambda i: (i, 0)), pl.BlockSpec((1, K), lambda i: (0, 0))],
        out_specs=pl.BlockSpec((tm, K), lambda i: (i, 0)),
        out_shape=jax.ShapeDtypeStruct((M, K), F32),
        compiler_params=_cparams(("parallel",)),
        name="final_norm",
    )(x, g.reshape(1, K))


def _conv_silu(x, tail_ref, w_ref, b_ref):
    L = x.shape[0]
    xp = jnp.concatenate([tail_ref[...], x], axis=0)
    y = b_ref[...] + w_ref[CONV_W - 1:CONV_W, :] * x
    for j in range(CONV_W - 1):
        shifted = pltpu.roll(xp, CONV_W - 1 - j, axis=0)[SUBLANES:SUBLANES + L]
        y = y + w_ref[j:j + 1, :] * shifted
    tail_ref[...] = x[L - SUBLANES:L]
    return _silu(y)


def _cumsum_rows(x):
    L = x.shape[0]
    row = lax.broadcasted_iota(jnp.int32, x.shape, 0)
    d = 1
    while d < L:
        x = x + jnp.where(row >= d, pltpu.roll(x, d, axis=0), 0.0)
        d *= 2
    return x


def _col_to_row(col, eye):
    return jnp.sum(jnp.where(eye, col, 0.0), axis=0, keepdims=True)


NN_DIMS = (((1,), (0,)), ((), ()))
NT_DIMS = (((1,), (1,)), ((), ()))
TN_DIMS = (((0,), (0,)), ((), ()))


def _mlstm_kernel(qk_ref, v_ref, o_ref, gt_ref, cw_ref, cb_ref, gb_ref, gout_ref, buf_ref,
                  C0_ref, n0_ref, m0_ref,
                  hn_ref, Cout_ref, nout_ref, mout_ref,
                  C_s, n_s, m_s, tail_s, *, nc, t_valid, precise):
    L = qk_ref.shape[0]
    c = pl.program_id(1)
    mm = functools.partial(_dotx, precise=precise)

    @pl.when(c == 0)
    def _():
        C_s[...] = C0_ref[...]
        n_s[...] = n0_ref[...]
        m_s[...] = m0_ref[...]
        tail_s[...] = buf_ref[...]

    qk = _conv_silu(qk_ref[...], tail_s, cw_ref, cb_ref)

    g = gt_ref[...] + gb_ref[...]
    lane = lax.broadcasted_iota(jnp.int32, g.shape, 1)
    logf = jnp.minimum(g, 0.0) - jnp.log1p(jnp.exp(-jnp.abs(g)))
    if t_valid < L:
        valid = lax.broadcasted_iota(jnp.int32, g.shape, 0) < t_valid
        logf = jnp.where(valid, logf, 0.0)
        g = jnp.where(valid, g, -jnp.inf)
    cum = _cumsum_rows(jnp.where(lane >= A_HEADS, logf, 0.0))

    ri = lax.broadcasted_iota(jnp.int32, (L, L), 0)
    ci = lax.broadcasted_iota(jnp.int32, (L, L), 1)
    eye = ri == ci
    causal = ri >= ci

    for h in range(A_HEADS):
        qf = qk[:, h * A_DQK:(h + 1) * A_DQK]
        kf = qk[:, A_QK + h * A_DQK:A_QK + (h + 1) * A_DQK] * (A_DQK ** -0.5)
        vf = v_ref[:, h * A_DV:(h + 1) * A_DV]
        igc = g[:, h:h + 1]
        cumc = cum[:, A_HEADS + h:A_HEADS + h + 1]
        m_prev = m_s[h:h + 1, 0:1]
        bcol = igc - cumc
        brow = _col_to_row(bcol, eye)
        log_intra = jnp.where(causal, cumc + brow, -jnp.inf)
        log_inter = cumc + m_prev
        m_t = jnp.maximum(log_inter, jnp.max(log_intra, axis=-1, keepdims=True))
        w_intra = jnp.exp(log_intra - m_t)
        w_inter = jnp.exp(log_inter - m_t)
        s = mm(qf, kf, NT_DIMS) * w_intra
        C_h = C_s[h]
        num = mm(s, vf, NN_DIMS) + mm(qf, C_h, NT_DIMS) * w_inter
        n_h = n_s[h:h + 1, :]
        den = jnp.sum(s, axis=-1, keepdims=True) + w_inter * jnp.sum(qf * n_h, axis=-1, keepdims=True)
        den = jnp.maximum(jnp.abs(den), jnp.exp(-m_t))
        hh = jax.nn.sigmoid(o_ref[:, h * A_DV:(h + 1) * A_DV]) * (num / den)
        hn_ref[:, h * A_DV:(h + 1) * A_DV] = (
            _rms_rows(hh) * gout_ref[:, h * A_DV:(h + 1) * A_DV]).astype(hn_ref.dtype)

        cum_l = cumc[L - 1:L, :]
        m_l = m_t[L - 1:L, :]
        w_last = jnp.exp(bcol + (cum_l - m_l))
        decay = jnp.exp(cum_l + m_prev - m_l)
        kw = w_last * kf
        upd = mm(vf, kw, TN_DIMS)
        C_s[h] = decay * C_h + upd
        n_s[h:h + 1, :] = decay * n_h + jnp.sum(kw, axis=0, keepdims=True)
        m_s[h:h + 1, :] = jnp.broadcast_to(m_l, (1, LANES))

    @pl.when(c == nc - 1)
    def _():
        Cout_ref[...] = C_s[...]
        nout_ref[...] = n_s[...]
        mout_ref[...] = m_s[...]


def _mlstm(proj, conv_w, conv_b, gate_b, g_out, buf8, C0, n0, m0, *, layer, Bsz, T, L, t_valid, precise=False):
    nc = T // L
    row = lambda b, c: b * nc + c
    per_b4 = lambda b, c: (b, 0, 0, 0)
    per_b3 = lambda b, c: (b, 0, 0)
    lay_b5 = lambda b, c: (layer, b, 0, 0, 0)
    lay_b4 = lambda b, c: (layer, b, 0, 0)
    const2 = lambda b, c: (0, 0)
    return pl.pallas_call(
        functools.partial(_mlstm_kernel, nc=nc, t_valid=t_valid, precise=precise),
        grid=(Bsz, nc),
        in_specs=[
            pl.BlockSpec((L, 2 * A_QK), lambda b, c: (row(b, c), 0)),
            pl.BlockSpec((L, A_VD), lambda b, c: (row(b, c), 1)),
            pl.BlockSpec((L, A_VD), lambda b, c: (row(b, c), 2)),
            pl.BlockSpec((L, LANES), lambda b, c: (row(b, c), A_GATE_BLK)),
            pl.BlockSpec((CONV_W, 2 * A_QK), const2),
            pl.BlockSpec((1, 2 * A_QK), const2),
            pl.BlockSpec((1, LANES), const2),
            pl.BlockSpec((1, A_VD), const2),
            pl.BlockSpec((None, SUBLANES, 2 * A_QK), per_b3),
            pl.BlockSpec((None, None, A_HEADS, A_DV, A_DQK), lay_b5),
            pl.BlockSpec((None, None, A_HEADS, A_DQK), lay_b4),
            pl.BlockSpec((None, None, A_HEADS, LANES), lay_b4),
        ],
        out_specs=[
            pl.BlockSpec((L, A_VD), lambda b, c: (row(b, c), 0)),
            pl.BlockSpec((None, A_HEADS, A_DV, A_DQK), per_b4),
            pl.BlockSpec((None, A_HEADS, A_DQK), per_b3),
            pl.BlockSpec((None, A_HEADS, LANES), per_b3),
        ],
        out_shape=[
            jax.ShapeDtypeStruct((Bsz * T, A_VD), F32 if precise else BF16),
            jax.ShapeDtypeStruct((Bsz, A_HEADS, A_DV, A_DQK), F32),
            jax.ShapeDtypeStruct((Bsz, A_HEADS, A_DQK), F32),
            jax.ShapeDtypeStruct((Bsz, A_HEADS, LANES), F32),
        ],
        scratch_shapes=[
            pltpu.VMEM((A_HEADS, A_DV, A_DQK), F32),
            pltpu.VMEM((A_HEADS, A_DQK), F32),
            pltpu.VMEM((A_HEADS, LANES), F32),
            pltpu.VMEM((SUBLANES, 2 * A_QK), F32),
        ],
        compiler_params=_cparams(("parallel", "arbitrary")),
        name="mlstm_scan",
    )(proj, proj, proj, proj, conv_w, conv_b, gate_b, g_out, buf8, C0, n0, m0)


def _ssd_kernel(z_ref, x_ref, bc_ref, dt_ref, cwx_ref, cbx_ref, cwbc_ref, cbbc_ref, dtb_ref, alog_ref,
                dskip_ref, gn_ref, bufx_ref, bufbc_ref, h0_ref,
                y_ref, hout_ref,
                h_s, tailx_s, tailbc_s, xw_s, *, nc, t_valid, precise):
    L = z_ref.shape[0]
    c = pl.program_id(1)
    mm = functools.partial(_dotx, precise=precise)

    @pl.when(c == 0)
    def _():
        h_s[...] = h0_ref[...]
        tailx_s[...] = bufx_ref[...]
        tailbc_s[...] = bufbc_ref[...]

    xs = _conv_silu(x_ref[...], tailx_s, cwx_ref, cbx_ref)
    bc = _conv_silu(bc_ref[...], tailbc_s, cwbc_ref, cbbc_ref)

    dt = _softplus(dt_ref[...] + dtb_ref[...])
    lane = lax.broadcasted_iota(jnp.int32, dt.shape, 1)
    keep = lane < B_HEADS
    if t_valid < L:
        keep = keep & (lax.broadcasted_iota(jnp.int32, dt.shape, 0) < t_valid)
    dt = jnp.where(keep, dt, 0.0)
    cum = _cumsum_rows(dt * (-jnp.exp(alog_ref[...])))
    ecum = jnp.exp(cum)
    cum_l = cum[L - 1:L, :]
    wl = jnp.exp(cum_l - cum)
    edecay = jnp.exp(cum_l)

    ri = lax.broadcasted_iota(jnp.int32, (L, L), 0)
    ci = lax.broadcasted_iota(jnp.int32, (L, L), 1)
    eye = ri == ci
    causal = ri >= ci
    lo = lax.broadcasted_iota(jnp.int32, (L, LANES), 1) < B_HEADDIM

    GN = B_GROUPS * B_STATE
    for g in range(B_GROUPS):
        Bm = bc[:, g * B_STATE:(g + 1) * B_STATE]
        Cm = bc[:, GN + g * B_STATE:GN + (g + 1) * B_STATE]
        cb = mm(Cm, Bm, NT_DIMS)
        h_g = h_s[g * B_GW:(g + 1) * B_GW, :]
        inter = mm(Cm, h_g, NT_DIMS)
        sq = jnp.zeros((L, 1), F32)
        ys = []
        for j in range(B_HPG // 2):
            col0 = g * B_GW + j * LANES
            xpair = xs[:, col0:col0 + LANES]
            hd0 = g * B_HPG + 2 * j
            xdt = xpair * jnp.where(lo, dt[:, hd0:hd0 + 1], dt[:, hd0 + 1:hd0 + 2])
            y = jnp.zeros((L, LANES), F32)
            for half in range(2):
                cumc = cum[:, hd0 + half:hd0 + half + 1]
                seg = jnp.exp(jnp.where(causal, cumc - _col_to_row(cumc, eye), -jnp.inf))
                xh = jnp.where(lo if half == 0 else ~lo, xdt, 0.0)
                y = y + mm(cb * seg, xh, NN_DIMS)
            e_pair = jnp.where(lo, ecum[:, hd0:hd0 + 1], ecum[:, hd0 + 1:hd0 + 2])
            w_pair = jnp.where(lo, wl[:, hd0:hd0 + 1], wl[:, hd0 + 1:hd0 + 2])
            y = y + e_pair * inter[:, j * LANES:(j + 1) * LANES]
            y = y + dskip_ref[:, col0:col0 + LANES] * xpair
            y = y * _silu(z_ref[:, col0:col0 + LANES])
            sq = sq + jnp.sum(y * y, axis=-1, keepdims=True)
            ys.append(y)
            xw_s[:, j * LANES:(j + 1) * LANES] = (xdt * w_pair).astype(xw_s.dtype)
        scale = lax.rsqrt(sq * (1.0 / B_GW) + EPS)
        for j in range(B_HPG // 2):
            col0 = g * B_GW + j * LANES
            y_ref[:, col0:col0 + LANES] = (ys[j] * scale * gn_ref[:, col0:col0 + LANES]).astype(y_ref.dtype)
        upd = mm(xw_s[...], Bm, TN_DIMS)
        for r in range(B_HPG):
            hd = g * B_HPG + r
            r0 = g * B_GW + r * B_HEADDIM
            h_s[r0:r0 + B_HEADDIM, :] = (edecay[:, hd:hd + 1] * h_g[r * B_HEADDIM:(r + 1) * B_HEADDIM, :]
                                         + upd[r * B_HEADDIM:(r + 1) * B_HEADDIM, :])

    @pl.when(c == nc - 1)
    def _():
        hout_ref[...] = h_s[...]


def _ssd(proj, cwx, cbx, cwbc, cbbc, dtb, alog, dskip, gnorm, bufx8, bufbc8, h0, *, layer, Bsz, T, L, t_valid,
         precise=False):
    nc = T // L
    GN2 = 2 * B_GROUPS * B_STATE
    row = lambda b, c: b * nc + c
    per_b3 = lambda b, c: (b, 0, 0)
    const2 = lambda b, c: (0, 0)
    return pl.pallas_call(
        functools.partial(_ssd_kernel, nc=nc, t_valid=t_valid, precise=precise),
        grid=(Bsz, nc),
        in_specs=[
            pl.BlockSpec((L, B_INNER), lambda b, c: (row(b, c), 0)),
            pl.BlockSpec((L, B_INNER), lambda b, c: (row(b, c), 1)),
            pl.BlockSpec((L, GN2), lambda b, c: (row(b, c), 2 * B_INNER // GN2)),
            pl.BlockSpec((L, LANES), lambda b, c: (row(b, c), B_DT_BLK)),
            pl.BlockSpec((CONV_W, B_INNER), const2),
            pl.BlockSpec((1, B_INNER), const2),
            pl.BlockSpec((CONV_W, GN2), const2),
            pl.BlockSpec((1, GN2), const2),
            pl.BlockSpec((1, LANES), const2),
            pl.BlockSpec((1, LANES), const2),
            pl.BlockSpec((1, B_INNER), const2),
            pl.BlockSpec((1, B_INNER), const2),
            pl.BlockSpec((None, SUBLANES, B_INNER), per_b3),
            pl.BlockSpec((None, SUBLANES, GN2), per_b3),
            pl.BlockSpec((None, None, B_INNER, B_STATE), lambda b, c: (layer, b, 0, 0)),
        ],
        out_specs=[
            pl.BlockSpec((L, B_INNER), lambda b, c: (row(b, c), 0)),
            pl.BlockSpec((None, B_INNER, B_STATE), per_b3),
        ],
        out_shape=[
            jax.ShapeDtypeStruct((Bsz * T, B_INNER), F32 if precise else BF16),
            jax.ShapeDtypeStruct((Bsz, B_INNER, B_STATE), F32),
        ],
        scratch_shapes=[
            pltpu.VMEM((B_INNER, B_STATE), F32),
            pltpu.VMEM((SUBLANES, B_INNER), F32),
            pltpu.VMEM((SUBLANES, GN2), F32),
            pltpu.VMEM((L, B_GW), F32 if precise else BF16),
        ],
        compiler_params=_cparams(("parallel", "arbitrary")),
        name="ssd_scan",
    )(proj, proj, proj, proj, cwx, cbx, cwbc, cbbc, dtb, alog, dskip, gnorm, bufx8, bufbc8, h0)


def _xattn_kernel(q_ref, k_ref, v_ref, o_ref, *, precise):
    mm = functools.partial(_dotx, precise=precise)
    for h in range(X_HEADS):
        sl = slice(h * X_HD, (h + 1) * X_HD)
        s = mm(q_ref[:, sl], k_ref[:, sl], NT_DIMS) * (X_HD ** -0.5)
        e = jnp.exp(s - jnp.max(s, axis=-1, keepdims=True))
        pv = mm(e, v_ref[:, sl], NN_DIMS)
        o_ref[:, sl] = (pv / jnp.sum(e, axis=-1, keepdims=True)).astype(o_ref.dtype)


def _xattn(q, k, v, *, layer, Bsz, T, tq, precise=False):
    nq = T // tq
    mem = k.shape[2]
    return pl.pallas_call(
        functools.partial(_xattn_kernel, precise=precise),
        grid=(Bsz, nq),
        in_specs=[
            pl.BlockSpec((tq, D_MODEL), lambda b, t: (b * nq + t, 0)),
            pl.BlockSpec((None, None, mem, D_MODEL), lambda b, t: (layer, b, 0, 0)),
            pl.BlockSpec((None, None, mem, D_MODEL), lambda b, t: (layer, b, 0, 0)),
        ],
        out_specs=pl.BlockSpec((tq, D_MODEL), lambda b, t: (b * nq + t, 0)),
        out_shape=jax.ShapeDtypeStruct((Bsz * T, D_MODEL), F32 if precise else BF16),
        compiler_params=_cparams(("parallel", "arbitrary")),
        name="xattn",
    )(q, k, v)


def _swiglu_kernel(*refs, nf, precise):
    if precise:
        x_ref, g_ref, wg_ref, wu_ref, wd_ref, wgl_ref, wul_ref, wdl_ref, o_ref, xs_ref, acc_ref, xl_ref = refs
    else:
        x_ref, g_ref, wg_ref, wu_ref, wd_ref, o_ref, xs_ref, acc_ref = refs
        wgl_ref = wul_ref = wdl_ref = xl_ref = None
    f = pl.program_id(1)

    @pl.when(f == 0)
    def _():
        xn = _rms_rows(x_ref[...]) * g_ref[...]
        if precise:
            xs_ref[...], xl_ref[...] = _hl(xn)
        else:
            xs_ref[...] = xn.astype(BF16)
        acc_ref[...] = jnp.zeros_like(acc_ref)

    xb = xs_ref[...]
    xl = xl_ref[...] if precise else None
    a = _dot_w(xb, xl, wg_ref, wgl_ref)
    u = _dot_w(xb, xl, wu_ref, wul_ref)
    hmid = _silu(a) * u
    hh, hl = _hl(hmid) if precise else (hmid.astype(BF16), None)
    acc_ref[...] += _dot_w(hh, hl, wd_ref, wdl_ref)

    @pl.when(f == nf - 1)
    def _():
        o_ref[...] = x_ref[...] + acc_ref[...]


def _swiglu(x, gain, wg, wu, wd, *, lo=None, tm, tf):
    M, K = x.shape
    F = wg.shape[1]
    tm = min(tm, M)
    nf = F // tf
    precise = lo is not None
    up = pl.BlockSpec((K, tf), lambda i, f: (0, f))
    down = pl.BlockSpec((tf, K), lambda i, f: (f, 0))
    return pl.pallas_call(
        functools.partial(_swiglu_kernel, nf=nf, precise=precise),
        grid=(M // tm, nf),
        in_specs=[pl.BlockSpec((tm, K), lambda i, f: (i, 0)), pl.BlockSpec((1, K), lambda i, f: (0, 0)),
                  up, up, down] + ([up, up, down] if precise else []),
        out_specs=pl.BlockSpec((tm, K), lambda i, f: (i, 0)),
        out_shape=jax.ShapeDtypeStruct((M, K), F32),
        scratch_shapes=[pltpu.VMEM((tm, K), BF16), pltpu.VMEM((tm, K), F32)]
        + ([pltpu.VMEM((tm, K), BF16)] if precise else []),
        compiler_params=_cparams(("parallel", "arbitrary")),
        name="swiglu",
    )(x, gain.reshape(1, K), wg, wu, wd, *(lo if precise else ()))


def _router_kernel(x_ref, g_ref, wr_ref, un_ref, idx_ref, wt_ref):
    un = _rms_rows(x_ref[...]) * g_ref[...]
    un_ref[...] = un
    logits = _dotx(un, wr_ref[...], NN_DIMS, True)
    lane = lax.broadcasted_iota(jnp.int32, logits.shape, 1)
    lg = jnp.where(lane < N_EXPERTS, logits, -jnp.inf)
    v1 = jnp.max(lg, axis=-1, keepdims=True)
    i1 = jnp.min(jnp.where(lg == v1, lane, LANES), axis=-1, keepdims=True)
    lg2 = jnp.where(lane == i1, -jnp.inf, lg)
    v2 = jnp.max(lg2, axis=-1, keepdims=True)
    i2 = jnp.min(jnp.where(lg2 == v2, lane, LANES), axis=-1, keepdims=True)
    e2 = jnp.exp(v2 - v1)
    w1 = 1.0 / (1.0 + e2)
    w2 = e2 / (1.0 + e2)
    idx_ref[...] = jnp.where(lane == 0, i1, jnp.where(lane == 1, i2, 0))
    wt_ref[...] = jnp.where(lane == 0, w1, jnp.where(lane == 1, w2, 0.0))


def _router(x, gain, wr_pad, tm):
    M, K = x.shape
    tm = min(tm, M)
    return pl.pallas_call(
        _router_kernel,
        grid=(M // tm,),
        in_specs=[
            pl.BlockSpec((tm, K), lambda i: (i, 0)),
            pl.BlockSpec((1, K), lambda i: (0, 0)),
            pl.BlockSpec((K, LANES), lambda i: (0, 0)),
        ],
        out_specs=[
            pl.BlockSpec((tm, K), lambda i: (i, 0)),
            pl.BlockSpec((tm, LANES), lambda i: (i, 0)),
            pl.BlockSpec((tm, LANES), lambda i: (i, 0)),
        ],
        out_shape=[
            jax.ShapeDtypeStruct((M, K), F32),
            jax.ShapeDtypeStruct((M, LANES), jnp.int32),
            jax.ShapeDtypeStruct((M, LANES), F32),
        ],
        compiler_params=_cparams(("parallel",)),
        name="moe_router",
    )(x, gain.reshape(1, K), wr_pad)


def _row_copy(src_hbm, row, dst_vmem, r, sem):
    return pltpu.make_async_copy(src_hbm.at[pl.ds(row, 1), :], dst_vmem.at[pl.ds(r, 1), :], sem)


def _moe_gather_kernel(src_ref, nu_ref, xp_hbm, xs_hbm, o_ref, sem, *, n_prompt):
    i = pl.program_id(0)
    tm = o_ref.shape[0]

    @pl.when(i < nu_ref[0])
    def _():
        def issue(r, carry):
            tok = src_ref[i * tm + r]

            @pl.when(tok < n_prompt)
            def _():
                _row_copy(xp_hbm, tok, o_ref, r, sem).start()

            @pl.when(tok >= n_prompt)
            def _():
                _row_copy(xs_hbm, tok - n_prompt, o_ref, r, sem).start()
            return carry

        lax.fori_loop(0, tm, issue, 0)
        pltpu.make_async_copy(xp_hbm.at[pl.ds(0, tm), :], o_ref, sem).wait()

    @pl.when(i >= nu_ref[0])
    def _():
        o_ref[...] = jnp.zeros_like(o_ref)


def _moe_gather(src_tok, n_used, un_p, un_s, n_tiles):
    K = un_p.shape[1]
    return pl.pallas_call(
        functools.partial(_moe_gather_kernel, n_prompt=un_p.shape[0]),
        grid_spec=pltpu.PrefetchScalarGridSpec(
            num_scalar_prefetch=2,
            grid=(n_tiles,),
            in_specs=[pl.BlockSpec(memory_space=pl.ANY), pl.BlockSpec(memory_space=pl.ANY)],
            out_specs=pl.BlockSpec((MOE_TM, K), lambda i, src, nu: (i, 0)),
            scratch_shapes=[pltpu.SemaphoreType.DMA(())],
        ),
        out_shape=jax.ShapeDtypeStruct((n_tiles * MOE_TM, K), F32),
        compiler_params=_cparams(("arbitrary",)),
        name="moe_gather",
    )(src_tok, n_used, un_p, un_s)


def _moe_ffn_kernel(te_ref, nu_ref, x_ref, wg_ref, wu_ref, wd_ref, o_ref, xs_ref, acc_ref, *, nf):
    i = pl.program_id(0)
    f = pl.program_id(1)

    @pl.when(i < nu_ref[0])
    def _():
        @pl.when(f == 0)
        def _():
            xs_ref[...] = x_ref[...].astype(BF16)
            acc_ref[...] = jnp.zeros_like(acc_ref)

        xb = xs_ref[...]
        a = jnp.dot(xb, wg_ref[...], preferred_element_type=F32)
        u = jnp.dot(xb, wu_ref[...], preferred_element_type=F32)
        acc_ref[...] += jnp.dot((_silu(a) * u).astype(BF16), wd_ref[...], preferred_element_type=F32)

        @pl.when(f == nf - 1)
        def _():
            o_ref[...] = acc_ref[...]

    @pl.when((i >= nu_ref[0]) & (f == 0))
    def _():
        o_ref[...] = jnp.zeros_like(o_ref)


def _moe_ffn(tile_expert, n_used, xs, wg, wu, wd, n_tiles):
    K = xs.shape[1]
    nf = D_FF_E // MOE_TF

    def tile(i, nu):
        return jnp.minimum(i, nu[0] - 1)

    def fsel(i, f, nu):
        return jnp.where(i < nu[0], f, nf - 1)

    return pl.pallas_call(
        functools.partial(_moe_ffn_kernel, nf=nf),
        grid_spec=pltpu.PrefetchScalarGridSpec(
            num_scalar_prefetch=2,
            grid=(n_tiles, nf),
            in_specs=[
                pl.BlockSpec((MOE_TM, K), lambda i, f, te, nu: (tile(i, nu), 0)),
                pl.BlockSpec((None, K, MOE_TF), lambda i, f, te, nu: (te[tile(i, nu)], 0, fsel(i, f, nu))),
                pl.BlockSpec((None, K, MOE_TF), lambda i, f, te, nu: (te[tile(i, nu)], 0, fsel(i, f, nu))),
                pl.BlockSpec((None, MOE_TF, K), lambda i, f, te, nu: (te[tile(i, nu)], fsel(i, f, nu), 0)),
            ],
            out_specs=pl.BlockSpec((MOE_TM, K), lambda i, f, te, nu: (i, 0)),
            scratch_shapes=[pltpu.VMEM((MOE_TM, K), BF16), pltpu.VMEM((MOE_TM, K), F32)],
        ),
        out_shape=jax.ShapeDtypeStruct((n_tiles * MOE_TM, K), F32),
        compiler_params=_cparams(("arbitrary", "arbitrary")),
        name="moe_ffn",
    )(tile_expert, n_used, xs, wg, wu, wd)


def _moe_combine_kernel(dest_ref, x_ref, wt_ref, ys_hbm, o_ref, buf, sem, *, row0):
    i = pl.program_id(0)
    tc = x_ref.shape[0]

    def issue(r, carry):
        t = row0 + i * tc + r
        _row_copy(ys_hbm, dest_ref[2 * t], buf.at[0], r, sem).start()
        _row_copy(ys_hbm, dest_ref[2 * t + 1], buf.at[1], r, sem).start()
        return carry

    lax.fori_loop(0, tc, issue, 0)
    pltpu.make_async_copy(ys_hbm.at[pl.ds(0, tc), :], buf.at[0], sem).wait()
    pltpu.make_async_copy(ys_hbm.at[pl.ds(0, tc), :], buf.at[1], sem).wait()
    wt = wt_ref[...]
    o_ref[...] = x_ref[...] + (wt[:, 0:1] * buf[0] + wt[:, 1:2] * buf[1])


def _moe_combine(dest, x, wts, ys, *, row0, tc):
    M, K = x.shape
    tc = min(tc, M)
    return pl.pallas_call(
        functools.partial(_moe_combine_kernel, row0=row0),
        grid_spec=pltpu.PrefetchScalarGridSpec(
            num_scalar_prefetch=1,
            grid=(M // tc,),
            in_specs=[
                pl.BlockSpec((tc, K), lambda i, d: (i, 0)),
                pl.BlockSpec((tc, LANES), lambda i, d: (i, 0)),
                pl.BlockSpec(memory_space=pl.ANY),
            ],
            out_specs=pl.BlockSpec((tc, K), lambda i, d: (i, 0)),
            scratch_shapes=[pltpu.VMEM((2, tc, K), F32), pltpu.SemaphoreType.DMA(())],
        ),
        out_shape=jax.ShapeDtypeStruct((M, K), F32),
        compiler_params=_cparams(("arbitrary",)),
        name="moe_combine",
    )(dest, x, wts, ys)


def _moe_plan(idx_all, n_tiles):
    N = idx_all.shape[0]
    sel = (idx_all[:, :, None] == jnp.arange(N_EXPERTS, dtype=jnp.int32)).astype(jnp.int32)
    per_tok = jnp.sum(sel, axis=1)
    before = jnp.cumsum(per_tok, axis=0) - per_tok
    counts = jnp.sum(per_tok, axis=0)
    padded = ((counts + MOE_TM - 1) // MOE_TM) * MOE_TM
    g_end = jnp.cumsum(padded)
    g_start = g_end - padded
    dest = jnp.sum(sel * (g_start + before)[:, None, :], axis=2)
    tok = jnp.broadcast_to(jnp.arange(N, dtype=jnp.int32)[:, None], (N, 2))
    src_tok = jnp.zeros((n_tiles * MOE_TM,), jnp.int32).at[dest.reshape(-1)].set(tok.reshape(-1))
    starts = jnp.arange(n_tiles, dtype=jnp.int32) * MOE_TM
    tile_expert = jnp.minimum(jnp.sum((starts[:, None] >= g_end[None, :]).astype(jnp.int32), axis=1),
                              N_EXPERTS - 1).astype(jnp.int32)
    n_used = (g_end[-1:] // MOE_TM).astype(jnp.int32)
    return dest.reshape(-1).astype(jnp.int32), src_tok, tile_expert, n_used


def _moe_layer(xp, xs, gain, wr_pad, wg, wu, wd):
    Np, Ns = xp.shape[0], xs.shape[0]
    un_p, idx_p, wt_p = _router(xp, gain, wr_pad, 1024)
    un_s, idx_s, wt_s = _router(xs, gain, wr_pad, 1024)
    idx_all = jnp.concatenate([idx_p[:, :2], idx_s[:, :2]], axis=0)
    n_tiles = (2 * (Np + Ns)) // MOE_TM + N_EXPERTS
    dest, src_tok, tile_expert, n_used = _moe_plan(idx_all, n_tiles)
    rows = _moe_gather(src_tok, n_used, un_p, un_s, n_tiles)
    ys = _moe_ffn(tile_expert, n_used, rows, wg, wu, wd, n_tiles)
    xp = _moe_combine(dest, xp, wt_p, ys, row0=0, tc=256)
    xs = _moe_combine(dest, xs, wt_s, ys, row0=Np, tc=256)
    return xp, xs


def _pad_cols(w, width):
    return jnp.pad(w, ((0, 0), (0, width - w.shape[1])))


def _tail8(buf):
    return jnp.pad(buf, ((0, 0), (SUBLANES - (CONV_W - 1), 0), (0, 0)))


def kernel(x_prompt, x_sample, mem_prompt, state_mlstm_C, state_mlstm_n, state_mlstm_m, state_mlstm_conv,
           state_ssd_h, state_ssd_conv, cache_mem_k, cache_mem_v, norm_mix, norm_xattn, norm_mem, norm_ffn,
           norm_final, a_w_in, a_conv_w, a_conv_b, a_b_i, a_b_f, a_norm, a_w_out, b_w_in, b_conv_w, b_conv_b,
           b_dt_bias, b_A_log, b_D, b_norm, b_w_out, x_wq, x_wk, x_wv, x_wo, f_w_gate, f_w_up, f_w_down,
           e_router, e_w_gate, e_w_up, e_w_down):
    Bp, Tp, D = x_prompt.shape
    Bs, Ts, _ = x_sample.shape
    Tpad = SAMPLE_PAD_T
    Np, Ns = Bp * Tp, Bs * Tpad
    mem_len = mem_prompt.shape[1]
    GN2 = 2 * B_GROUPS * B_STATE

    groups = {
        "p": dict(B=Bp, T=Tp, L=SCAN_L, tv=SCAN_L, tm=1024, tq=512),
        "s": dict(B=Bs, T=Tpad, L=Tpad, tv=Ts, tm=1024, tq=Tpad),
    }
    xres = {
        "p": x_prompt.reshape(Np, D),
        "s": jnp.pad(x_sample, ((0, 0), (0, Tpad - Ts), (0, 0))).reshape(Ns, D),
    }
    mem_kv = {"s": (cache_mem_k.reshape(DEPTH, Bs, mem_len, D), cache_mem_v.reshape(DEPTH, Bs, mem_len, D))}
    mem_flat = mem_prompt.reshape(Bp * mem_len, D)

    st = {
        "p": dict(
            C=jnp.zeros((2, Bp, A_HEADS, A_DV, A_DQK), F32), n=jnp.zeros((2, Bp, A_HEADS, A_DQK), F32),
            m=jnp.full((2, Bp, A_HEADS), M_INIT, F32), aconv=jnp.zeros((2, Bp, CONV_W - 1, 2 * A_QK), F32),
            h=jnp.zeros((2, Bp, B_HEADS, B_HEADDIM, B_STATE), F32),
            bconv=jnp.zeros((2, Bp, CONV_W - 1, B_INNER + GN2), F32)),
        "s": dict(C=state_mlstm_C, n=state_mlstm_n, m=state_mlstm_m, aconv=state_mlstm_conv,
                  h=state_ssd_h, bconv=state_ssd_conv),
    }
    new = {k: dict(C=[], n=[], m=[], aconv=[], h=[], bconv=[]) for k in groups}
    pk, pv = [], []

    def mmw(k, hp, x, w, **kw):
        if k == "p" and hp:
            hi, lo = _split_bf16(w)
            return _mm(x, hi, w_lo=lo, **dict(kw, tm=PRECISE_TM))
        return _mm(x, w.astype(BF16), **kw)

    for i in range(DEPTH):
        j = i // 2
        hp = i < PRECISE_LAYERS
        if i % 2 == 0:
            w_in = _pad_cols(a_w_in[j], A_PROJ_W)
            gate_b = _pad_cols(jnp.concatenate([a_b_i[j], a_b_f[j]])[None, :], LANES)
            for k, cfg in groups.items():
                B, T = cfg["B"], cfg["T"]
                proj = mmw(k, hp, xres[k], w_in, gain=norm_mix[i], tm=cfg["tm"], tn=1280, name="a_proj")
                s0 = st[k]
                hn, C, n, m = _mlstm(
                    proj, a_conv_w[j], a_conv_b[j][None, :], gate_b, a_norm[j][None, :], _tail8(s0["aconv"][j]),
                    s0["C"], s0["n"], jnp.broadcast_to(s0["m"][..., None], s0["m"].shape + (LANES,)),
                    layer=j, Bsz=B, T=T, L=cfg["L"], t_valid=cfg["tv"], precise=hp and k == "p")
                xres[k] = mmw(k, hp, hn, a_w_out[j], res=xres[k], tm=cfg["tm"], tn=1024, name="a_out")
                tv = T if k == "p" else Ts
                raw = proj.reshape(B, T, A_PROJ_W)[:, tv - (CONV_W - 1):tv, :2 * A_QK]
                new[k]["C"].append(C); new[k]["n"].append(n); new[k]["m"].append(m[..., 0])
                new[k]["aconv"].append(raw)
        else:
            w_in = _pad_cols(b_w_in[j], B_PROJ_W)
            cw, cbias = b_conv_w[j], b_conv_b[j][None, :]
            dtb = _pad_cols(b_dt_bias[j][None, :], LANES)
            alog = _pad_cols(b_A_log[j][None, :], LANES)
            dskip = jnp.repeat(b_D[j], B_HEADDIM)[None, :]
            for k, cfg in groups.items():
                B, T = cfg["B"], cfg["T"]
                proj = mmw(k, hp, xres[k], w_in, gain=norm_mix[i], tm=cfg["tm"], tn=1792, name="b_proj")
                s0 = st[k]
                buf8 = _tail8(s0["bconv"][j])
                yz, h = _ssd(
                    proj, cw[:, :B_INNER], cbias[:, :B_INNER], cw[:, B_INNER:], cbias[:, B_INNER:], dtb, alog,
                    dskip, b_norm[j][None, :], buf8[..., :B_INNER], buf8[..., B_INNER:],
                    s0["h"].reshape(-1, B, B_INNER, B_STATE), layer=j, Bsz=B, T=T, L=cfg["L"],
                    t_valid=cfg["tv"], precise=hp and k == "p")
                xres[k] = mmw(k, hp, yz, b_w_out[j], res=xres[k], tm=cfg["tm"], tn=1024, name="b_out")
                tv = T if k == "p" else Ts
                raw = proj.reshape(B, T, B_PROJ_W)[:, tv - (CONV_W - 1):tv, B_INNER:2 * B_INNER + GN2]
                new[k]["h"].append(h.reshape(B, B_HEADS, B_HEADDIM, B_STATE))
                new[k]["bconv"].append(raw)

        kp = mmw("p", hp, mem_flat, x_wk[i], gain=norm_mem[i], tm=1024, tn=1024, name="mem_k")
        vp = mmw("p", hp, mem_flat, x_wv[i], gain=norm_mem[i], tm=1024, tn=1024, name="mem_v")
        pk.append(kp); pv.append(vp)
        for k, cfg in groups.items():
            B, T = cfg["B"], cfg["T"]
            pr = hp and k == "p"
            q = mmw(k, hp, xres[k], x_wq[i], gain=norm_xattn[i], tm=cfg["tm"], tn=1024,
                    out_dtype=F32 if pr else BF16, name="x_q")
            if k == "p":
                kk, vv, lay = kp.reshape(1, B, mem_len, D), vp.reshape(1, B, mem_len, D), 0
            else:
                kk, vv, lay = mem_kv["s"][0], mem_kv["s"][1], i
            o = _xattn(q, kk, vv, layer=lay, Bsz=B, T=T, tq=cfg["tq"], precise=pr)
            xres[k] = mmw(k, hp, o, x_wo[i], res=xres[k], tm=cfg["tm"], tn=1024, name="x_o")

        if i % 2 == 0:
            ws = (f_w_gate[j], f_w_up[j], f_w_down[j])
            for k, cfg in groups.items():
                if hp and k == "p":
                    parts = [_split_bf16(w) for w in ws]
                    xres[k] = _swiglu(xres[k], norm_ffn[i], *[p[0] for p in parts], lo=tuple(p[1] for p in parts),
                                      tm=512, tf=256)
                else:
                    xres[k] = _swiglu(xres[k], norm_ffn[i], *[w.astype(BF16) for w in ws], tm=512, tf=1408)
        else:
            xres["p"], xres["s"] = _moe_layer(
                xres["p"], xres["s"], norm_ffn[i], _pad_cols(e_router[j], LANES),
                e_w_gate[j].astype(BF16), e_w_up[j].astype(BF16), e_w_down[j].astype(BF16))

    y_p = _final_norm(xres["p"], norm_final, 1024).reshape(Bp, Tp, D)
    y_s = _final_norm(xres["s"], norm_final, 1024).reshape(Bs, Tpad, D)[:, :Ts]

    def stack(k, name):
        return jnp.stack(new[k][name])

    p_mem_k = jnp.stack(pk).reshape(DEPTH, Bp, mem_len, X_HEADS, X_HD)
    p_mem_v = jnp.stack(pv).reshape(DEPTH, Bp, mem_len, X_HEADS, X_HD)
    return (y_p, y_s,
            stack("p", "C"), stack("p", "n"), stack("p", "m"), stack("p", "aconv"), stack("p", "h"),
            stack("p", "bconv"), p_mem_k, p_mem_v,
            stack("s", "C"), stack("s", "n"), stack("s", "m"), stack("s", "aconv"), stack("s", "h"),
            stack("s", "bconv"))
```

```python
import functools

import jax
import jax.numpy as jnp
from jax import lax
from jax.experimental import pallas as pl
from jax.experimental.pallas import tpu as pltpu

F32 = jnp.float32
BF16 = jnp.bfloat16

D_MODEL = 1024
DEPTH = 4
CONV_W = 4
EPS = 1e-6
M_INIT = -1e30
A_HEADS = 4
A_DQK = 256
A_DV = 512
A_QK = A_HEADS * A_DQK
A_VD = A_HEADS * A_DV
B_INNER = 2 * D_MODEL
B_HEADDIM = 64
B_HEADS = B_INNER // B_HEADDIM
B_GROUPS = 4
B_HPG = B_HEADS // B_GROUPS
B_STATE = 128
B_GW = B_HPG * B_HEADDIM
X_HEADS = 4
X_HD = D_MODEL // X_HEADS
N_EXPERTS = 8
D_FF_E = 3584

LANES = 128
SUBLANES = 8
VMEM_LIMIT_BYTES = 56 * 1024 * 1024

A_PROJ_W = 2 * A_QK + 2 * A_VD + 256
B_PROJ_W = 2 * B_INNER + 2 * B_GROUPS * B_STATE + 256
A_GATE_BLK = (2 * A_QK + 2 * A_VD) // LANES
B_DT_BLK = (2 * B_INNER + 2 * B_GROUPS * B_STATE) // LANES

SAMPLE_PAD_T = 8
MOE_TM = 512
MOE_TF = 1792
SCAN_L = 256
PRECISE_LAYERS = 2
PRECISE_TM = 512


def _cparams(sem):
    return pltpu.CompilerParams(dimension_semantics=sem, vmem_limit_bytes=VMEM_LIMIT_BYTES)


def _silu(x):
    return x * jax.nn.sigmoid(x)


def _softplus(x):
    return jnp.maximum(x, 0.0) + jnp.log1p(jnp.exp(-jnp.abs(x)))


def _rms_rows(x):
    return x * lax.rsqrt(jnp.mean(x * x, axis=-1, keepdims=True) + EPS)


def _hl(x):
    hi = x.astype(BF16)
    return hi, (x - hi.astype(F32)).astype(BF16)


def _dotx(a, b, dims, precise):
    d = lambda u, v: lax.dot_general(u, v, dims, preferred_element_type=F32)
    if not precise:
        return d(a.astype(BF16), b.astype(BF16))
    ah, al = _hl(a.astype(F32))
    bh, bl = _hl(b.astype(F32))
    return d(ah, bh) + (d(al, bh) + d(ah, bl))


def _dot_w(xh, xl, w_ref, wl_ref):
    acc = jnp.dot(xh, w_ref[...], preferred_element_type=F32)
    if wl_ref is not None:
        acc = acc + (jnp.dot(xl, w_ref[...], preferred_element_type=F32)
                     + jnp.dot(xh, wl_ref[...], preferred_element_type=F32))
    return acc


def _mm_kernel(*refs, has_gain, has_res, stage, precise):
    it = iter(refs)
    x_ref = next(it)
    w_ref = next(it)
    wl_ref = next(it) if precise else None
    g_ref = next(it) if has_gain else None
    r_ref = next(it) if has_res else None
    o_ref = next(it)
    xs_ref = next(it) if stage else None
    xl_ref = next(it) if precise else None

    if stage:
        @pl.when(pl.program_id(1) == 0)
        def _():
            x = x_ref[...].astype(F32)
            if has_gain:
                x = _rms_rows(x) * g_ref[...]
            if precise:
                xs_ref[...], xl_ref[...] = _hl(x)
            else:
                xs_ref[...] = x.astype(BF16)
        lhs = xs_ref[...]
    else:
        lhs = x_ref[...]
    acc = _dot_w(lhs, xl_ref[...] if precise else None, w_ref, wl_ref)
    if has_res:
        acc = acc + r_ref[...]
    o_ref[...] = acc.astype(o_ref.dtype)


def _mm(x, w, *, w_lo=None, gain=None, res=None, tm, tn, out_dtype=F32, name="mm"):
    M, K = x.shape
    N = w.shape[1]
    tm = min(tm, M)
    assert M % tm == 0 and N % tn == 0, (M, tm, N, tn)
    precise = w_lo is not None
    stage = precise or gain is not None or x.dtype != BF16
    wspec = pl.BlockSpec((K, tn), lambda i, j: (0, j))
    ins = [x, w]
    specs = [pl.BlockSpec((tm, K), lambda i, j: (i, 0)), wspec]
    if precise:
        ins.append(w_lo)
        specs.append(wspec)
    if gain is not None:
        ins.append(gain.reshape(1, K).astype(F32))
        specs.append(pl.BlockSpec((1, K), lambda i, j: (0, 0)))
    if res is not None:
        ins.append(res)
        specs.append(pl.BlockSpec((tm, tn), lambda i, j: (i, j)))
    scratch = [pltpu.VMEM((tm, K), BF16)] * (2 if precise else 1) if stage else []
    return pl.pallas_call(
        functools.partial(_mm_kernel, has_gain=gain is not None, has_res=res is not None, stage=stage,
                          precise=precise),
        grid=(M // tm, N // tn),
        in_specs=specs,
        out_specs=pl.BlockSpec((tm, tn), lambda i, j: (i, j)),
        out_shape=jax.ShapeDtypeStruct((M, N), out_dtype),
        scratch_shapes=scratch,
        compiler_params=_cparams(("parallel", "arbitrary")),
        name=name,
    )(*ins)


def _split_bf16(w):
    bits = lax.bitcast_convert_type(w, jnp.uint32)
    hi = lax.bitcast_convert_type(bits & jnp.uint32(0xFFFF0000), F32)
    return hi.astype(BF16), (w - hi).astype(BF16)


def _norm_kernel(x_ref, g_ref, o_ref):
    o_ref[...] = _rms_rows(x_ref[...]) * g_ref[...]


def _final_norm(x, g, tm):
    M, K = x.shape
    tm = min(tm, M)
    return pl.pallas_call(
        _norm_kernel,
        grid=(M // tm,),
        in_specs=[pl.BlockSpec((tm, K), lambda i: (i, 0)), pl.BlockSpec((1, K), lambda i: (0, 0))],
        out_specs=pl.BlockSpec((tm, K), lambda i: (i, 0)),
        out_shape=jax.ShapeDtypeStruct((M, K), F32),
        compiler_params=_cparams(("parallel",)),
        name="final_norm",
    )(x, g.reshape(1, K))


def _conv_silu(x, tail_ref, w_ref, b_ref):
    L = x.shape[0]
    xp = jnp.concatenate([tail_ref[...], x], axis=0)
    y = b_ref[...] + w_ref[CONV_W - 1:CONV_W, :] * x
    for j in range(CONV_W - 1):
        shifted = pltpu.roll(xp, CONV_W - 1 - j, axis=0)[SUBLANES:SUBLANES + L]
        y = y + w_ref[j:j + 1, :] * shifted
    tail_ref[...] = x[L - SUBLANES:L]
    return _silu(y)


def _cumsum_rows(x):
    L = x.shape[0]
    row = lax.broadcasted_iota(jnp.int32, x.shape, 0)
    d = 1
    while d < L:
        x = x + jnp.where(row >= d, pltpu.roll(x, d, axis=0), 0.0)
        d *= 2
    return x


def _col_to_row(col, eye):
    return jnp.sum(jnp.where(eye, col, 0.0), axis=0, keepdims=True)


def _store_state(own_slot, out_refs, state_refs):
    for o_ref, s_ref in zip(out_refs, state_refs):
        if own_slot is None:
            o_ref[...] = s_ref[...]
        else:
            for l in range(o_ref.shape[0]):
                o_ref[l] = s_ref[...] if l == own_slot else jnp.zeros(s_ref.shape, s_ref.dtype)


def _state_specs(shapes, n_layers, layer, Bsz, prev):
    def spec(shape):
        zeros = (0,) * len(shape)
        if prev is None:
            return pl.BlockSpec((n_layers, None) + shape, lambda b, c: (0, b) + zeros)
        return pl.BlockSpec((None, None) + shape, lambda b, c: (layer, b) + zeros)
    return ([spec(s) for s in shapes],
            [jax.ShapeDtypeStruct((n_layers, Bsz) + s, F32) for s in shapes])


NN_DIMS = (((1,), (0,)), ((), ()))
NT_DIMS = (((1,), (1,)), ((), ()))
TN_DIMS = (((0,), (0,)), ((), ()))


def _mlstm_kernel(qk_ref, v_ref, o_ref, gt_ref, cw_ref, cb_ref, gb_ref, gout_ref, buf_ref,
                  C0_ref, n0_ref, m0_ref, *rest, nc, t_valid, precise, own_slot):
    hn_ref, Cout_ref, nout_ref, mout_ref, C_s, n_s, m_s, tail_s = rest[-8:]
    L = qk_ref.shape[0]
    c = pl.program_id(1)
    mm = functools.partial(_dotx, precise=precise)

    @pl.when(c == 0)
    def _():
        C_s[...] = C0_ref[...]
        n_s[...] = n0_ref[...]
        m_s[...] = m0_ref[...]
        tail_s[...] = buf_ref[...]

    qk = _conv_silu(qk_ref[...], tail_s, cw_ref, cb_ref)

    g = gt_ref[...] + gb_ref[...]
    lane = lax.broadcasted_iota(jnp.int32, g.shape, 1)
    logf = jnp.minimum(g, 0.0) - jnp.log1p(jnp.exp(-jnp.abs(g)))
    if t_valid < L:
        valid = lax.broadcasted_iota(jnp.int32, g.shape, 0) < t_valid
        logf = jnp.where(valid, logf, 0.0)
        g = jnp.where(valid, g, -jnp.inf)
    cum = _cumsum_rows(jnp.where(lane >= A_HEADS, logf, 0.0))

    ri = lax.broadcasted_iota(jnp.int32, (L, L), 0)
    ci = lax.broadcasted_iota(jnp.int32, (L, L), 1)
    eye = ri == ci
    causal = ri >= ci

    for h in range(A_HEADS):
        qf = qk[:, h * A_DQK:(h + 1) * A_DQK]
        kf = qk[:, A_QK + h * A_DQK:A_QK + (h + 1) * A_DQK] * (A_DQK ** -0.5)
        vf = v_ref[:, h * A_DV:(h + 1) * A_DV]
        igc = g[:, h:h + 1]
        cumc = cum[:, A_HEADS + h:A_HEADS + h + 1]
        m_prev = m_s[h:h + 1, 0:1]
        bcol = igc - cumc
        brow = _col_to_row(bcol, eye)
        log_intra = jnp.where(causal, cumc + brow, -jnp.inf)
        log_inter = cumc + m_prev
        m_t = jnp.maximum(log_inter, jnp.max(log_intra, axis=-1, keepdims=True))
        w_intra = jnp.exp(log_intra - m_t)
        w_inter = jnp.exp(log_inter - m_t)
        s = mm(qf, kf, NT_DIMS) * w_intra
        C_h = C_s[h]
        num = mm(s, vf, NN_DIMS) + mm(qf, C_h, NT_DIMS) * w_inter
        n_h = n_s[h:h + 1, :]
        den = jnp.sum(s, axis=-1, keepdims=True) + w_inter * jnp.sum(qf * n_h, axis=-1, keepdims=True)
        den = jnp.maximum(jnp.abs(den), jnp.exp(-m_t))
        hh = jax.nn.sigmoid(o_ref[:, h * A_DV:(h + 1) * A_DV]) * (num / den)
        hn_ref[:, h * A_DV:(h + 1) * A_DV] = (
            _rms_rows(hh) * gout_ref[:, h * A_DV:(h + 1) * A_DV]).astype(hn_ref.dtype)

        cum_l = cumc[L - 1:L, :]
        m_l = m_t[L - 1:L, :]
        w_last = jnp.exp(bcol + (cum_l - m_l))
        decay = jnp.exp(cum_l + m_prev - m_l)
        kw = w_last * kf
        upd = mm(vf, kw, TN_DIMS)
        C_s[h] = decay * C_h + upd
        n_s[h:h + 1, :] = decay * n_h + jnp.sum(kw, axis=0, keepdims=True)
        m_s[h:h + 1, :] = jnp.broadcast_to(m_l, (1, LANES))

    @pl.when(c == nc - 1)
    def _():
        _store_state(own_slot, (Cout_ref, nout_ref, mout_ref), (C_s, n_s, m_s))


def _mlstm(proj, conv_w, conv_b, gate_b, g_out, buf8, C0, n0, m0, *, layer, Bsz, T, L, t_valid, precise=False,
           prev=None):
    nc = T // L
    st_specs, st_shapes = _state_specs([(A_HEADS, A_DV, A_DQK), (A_HEADS, A_DQK), (A_HEADS, LANES)],
                                       C0.shape[0], layer, Bsz, prev)
    n_in = 12
    row = lambda b, c: b * nc + c
    per_b3 = lambda b, c: (b, 0, 0)
    lay_b5 = lambda b, c: (layer, b, 0, 0, 0)
    lay_b4 = lambda b, c: (layer, b, 0, 0)
    const2 = lambda b, c: (0, 0)
    return pl.pallas_call(
        functools.partial(_mlstm_kernel, nc=nc, t_valid=t_valid, precise=precise,
                          own_slot=layer if prev is None else None),
        grid=(Bsz, nc),
        in_specs=[
            pl.BlockSpec((L, 2 * A_QK), lambda b, c: (row(b, c), 0)),
            pl.BlockSpec((L, A_VD), lambda b, c: (row(b, c), 1)),
            pl.BlockSpec((L, A_VD), lambda b, c: (row(b, c), 2)),
            pl.BlockSpec((L, LANES), lambda b, c: (row(b, c), A_GATE_BLK)),
            pl.BlockSpec((CONV_W, 2 * A_QK), const2),
            pl.BlockSpec((1, 2 * A_QK), const2),
            pl.BlockSpec((1, LANES), const2),
            pl.BlockSpec((1, A_VD), const2),
            pl.BlockSpec((None, SUBLANES, 2 * A_QK), per_b3),
            pl.BlockSpec((None, None, A_HEADS, A_DV, A_DQK), lay_b5),
            pl.BlockSpec((None, None, A_HEADS, A_DQK), lay_b4),
            pl.BlockSpec((None, None, A_HEADS, LANES), lay_b4),
        ] + [pl.BlockSpec(memory_space=pl.ANY)] * (0 if prev is None else 3),
        out_specs=[pl.BlockSpec((L, A_VD), lambda b, c: (row(b, c), 0))] + st_specs,
        out_shape=[jax.ShapeDtypeStruct((Bsz * T, A_VD), F32 if precise else BF16)] + st_shapes,
        input_output_aliases={} if prev is None else {n_in: 1, n_in + 1: 2, n_in + 2: 3},
        scratch_shapes=[
            pltpu.VMEM((A_HEADS, A_DV, A_DQK), F32),
            pltpu.VMEM((A_HEADS, A_DQK), F32),
            pltpu.VMEM((A_HEADS, LANES), F32),
            pltpu.VMEM((SUBLANES, 2 * A_QK), F32),
        ],
        compiler_params=_cparams(("parallel", "arbitrary")),
        name="mlstm_scan",
    )(proj, proj, proj, proj, conv_w, conv_b, gate_b, g_out, buf8, C0, n0, m0, *(prev or ()))


def _ssd_kernel(z_ref, x_ref, bc_ref, dt_ref, cwx_ref, cbx_ref, cwbc_ref, cbbc_ref, dtb_ref, alog_ref,
                dskip_ref, gn_ref, bufx_ref, bufbc_ref, h0_ref, *rest, nc, t_valid, precise, own_slot):
    y_ref, hout_ref, h_s, tailx_s, tailbc_s, xw_s = rest[-6:]
    L = z_ref.shape[0]
    c = pl.program_id(1)
    mm = functools.partial(_dotx, precise=precise)

    @pl.when(c == 0)
    def _():
        h_s[...] = h0_ref[...]
        tailx_s[...] = bufx_ref[...]
        tailbc_s[...] = bufbc_ref[...]

    xs = _conv_silu(x_ref[...], tailx_s, cwx_ref, cbx_ref)
    bc = _conv_silu(bc_ref[...], tailbc_s, cwbc_ref, cbbc_ref)

    dt = _softplus(dt_ref[...] + dtb_ref[...])
    lane = lax.broadcasted_iota(jnp.int32, dt.shape, 1)
    keep = lane < B_HEADS
    if t_valid < L:
        keep = keep & (lax.broadcasted_iota(jnp.int32, dt.shape, 0) < t_valid)
    dt = jnp.where(keep, dt, 0.0)
    cum = _cumsum_rows(dt * (-jnp.exp(alog_ref[...])))
    ecum = jnp.exp(cum)
    cum_l = cum[L - 1:L, :]
    wl = jnp.exp(cum_l - cum)
    edecay = jnp.exp(cum_l)

    ri = lax.broadcasted_iota(jnp.int32, (L, L), 0)
    ci = lax.broadcasted_iota(jnp.int32, (L, L), 1)
    eye = ri == ci
    causal = ri >= ci
    lo = lax.broadcasted_iota(jnp.int32, (L, LANES), 1) < B_HEADDIM

    GN = B_GROUPS * B_STATE
    for g in range(B_GROUPS):
        Bm = bc[:, g * B_STATE:(g + 1) * B_STATE]
        Cm = bc[:, GN + g * B_STATE:GN + (g + 1) * B_STATE]
        cb = mm(Cm, Bm, NT_DIMS)
        h_g = h_s[g * B_GW:(g + 1) * B_GW, :]
        inter = mm(Cm, h_g, NT_DIMS)
        sq = jnp.zeros((L, 1), F32)
        ys = []
        for j in range(B_HPG // 2):
            col0 = g * B_GW + j * LANES
            xpair = xs[:, col0:col0 + LANES]
            hd0 = g * B_HPG + 2 * j
            xdt = xpair * jnp.where(lo, dt[:, hd0:hd0 + 1], dt[:, hd0 + 1:hd0 + 2])
            y = jnp.zeros((L, LANES), F32)
            for half in range(2):
                cumc = cum[:, hd0 + half:hd0 + half + 1]
                seg = jnp.exp(jnp.where(causal, cumc - _col_to_row(cumc, eye), -jnp.inf))
                xh = jnp.where(lo if half == 0 else ~lo, xdt, 0.0)
                y = y + mm(cb * seg, xh, NN_DIMS)
            e_pair = jnp.where(lo, ecum[:, hd0:hd0 + 1], ecum[:, hd0 + 1:hd0 + 2])
            w_pair = jnp.where(lo, wl[:, hd0:hd0 + 1], wl[:, hd0 + 1:hd0 + 2])
            y = y + e_pair * inter[:, j * LANES:(j + 1) * LANES]
            y = y + dskip_ref[:, col0:col0 + LANES] * xpair
            y = y * _silu(z_ref[:, col0:col0 + LANES])
            sq = sq + jnp.sum(y * y, axis=-1, keepdims=True)
            ys.append(y)
            xw_s[:, j * LANES:(j + 1) * LANES] = (xdt * w_pair).astype(xw_s.dtype)
        scale = lax.rsqrt(sq * (1.0 / B_GW) + EPS)
        for j in range(B_HPG // 2):
            col0 = g * B_GW + j * LANES
            y_ref[:, col0:col0 + LANES] = (ys[j] * scale * gn_ref[:, col0:col0 + LANES]).astype(y_ref.dtype)
        upd = mm(xw_s[...], Bm, TN_DIMS)
        for r in range(B_HPG):
            hd = g * B_HPG + r
            r0 = g * B_GW + r * B_HEADDIM
            h_s[r0:r0 + B_HEADDIM, :] = (edecay[:, hd:hd + 1] * h_g[r * B_HEADDIM:(r + 1) * B_HEADDIM, :]
                                         + upd[r * B_HEADDIM:(r + 1) * B_HEADDIM, :])

    @pl.when(c == nc - 1)
    def _():
        _store_state(own_slot, (hout_ref,), (h_s,))


def _ssd(proj, cwx, cbx, cwbc, cbbc, dtb, alog, dskip, gnorm, bufx8, bufbc8, h0, *, layer, Bsz, T, L, t_valid,
         precise=False, prev=None):
    nc = T // L
    st_specs, st_shapes = _state_specs([(B_INNER, B_STATE)], h0.shape[0], layer, Bsz, prev)
    n_in = 15
    GN2 = 2 * B_GROUPS * B_STATE
    row = lambda b, c: b * nc + c
    per_b3 = lambda b, c: (b, 0, 0)
    const2 = lambda b, c: (0, 0)
    return pl.pallas_call(
        functools.partial(_ssd_kernel, nc=nc, t_valid=t_valid, precise=precise,
                          own_slot=layer if prev is None else None),
        grid=(Bsz, nc),
        in_specs=[
            pl.BlockSpec((L, B_INNER), lambda b, c: (row(b, c), 0)),
            pl.BlockSpec((L, B_INNER), lambda b, c: (row(b, c), 1)),
            pl.BlockSpec((L, GN2), lambda b, c: (row(b, c), 2 * B_INNER // GN2)),
            pl.BlockSpec((L, LANES), lambda b, c: (row(b, c), B_DT_BLK)),
            pl.BlockSpec((CONV_W, B_INNER), const2),
            pl.BlockSpec((1, B_INNER), const2),
            pl.BlockSpec((CONV_W, GN2), const2),
            pl.BlockSpec((1, GN2), const2),
            pl.BlockSpec((1, LANES), const2),
            pl.BlockSpec((1, LANES), const2),
            pl.BlockSpec((1, B_INNER), const2),
            pl.BlockSpec((1, B_INNER), const2),
            pl.BlockSpec((None, SUBLANES, B_INNER), per_b3),
            pl.BlockSpec((None, SUBLANES, GN2), per_b3),
            pl.BlockSpec((None, None, B_INNER, B_STATE), lambda b, c: (layer, b, 0, 0)),
        ] + [pl.BlockSpec(memory_space=pl.ANY)] * (0 if prev is None else 1),
        out_specs=[pl.BlockSpec((L, B_INNER), lambda b, c: (row(b, c), 0))] + st_specs,
        out_shape=[jax.ShapeDtypeStruct((Bsz * T, B_INNER), F32 if precise else BF16)] + st_shapes,
        input_output_aliases={} if prev is None else {n_in: 1},
        scratch_shapes=[
            pltpu.VMEM((B_INNER, B_STATE), F32),
            pltpu.VMEM((SUBLANES, B_INNER), F32),
            pltpu.VMEM((SUBLANES, GN2), F32),
            pltpu.VMEM((L, B_GW), F32 if precise else BF16),
        ],
        compiler_params=_cparams(("parallel", "arbitrary")),
        name="ssd_scan",
    )(proj, proj, proj, proj, cwx, cbx, cwbc, cbbc, dtb, alog, dskip, gnorm, bufx8, bufbc8, h0, *(prev or ()))


def _xattn_kernel(q_ref, k_ref, v_ref, o_ref, *, precise):
    mm = functools.partial(_dotx, precise=precise)
    for h in range(X_HEADS):
        sl = slice(h * X_HD, (h + 1) * X_HD)
        s = mm(q_ref[:, sl], k_ref[:, sl], NT_DIMS) * (X_HD ** -0.5)
        e = jnp.exp(s - jnp.max(s, axis=-1, keepdims=True))
        pv = mm(e, v_ref[:, sl], NN_DIMS)
        o_ref[:, sl] = (pv / jnp.sum(e, axis=-1, keepdims=True)).astype(o_ref.dtype)


def _xattn(q, k, v, *, layer, Bsz, T, tq, precise=False):
    nq = T // tq
    mem = k.shape[2]
    return pl.pallas_call(
        functools.partial(_xattn_kernel, precise=precise),
        grid=(Bsz, nq),
        in_specs=[
            pl.BlockSpec((tq, D_MODEL), lambda b, t: (b * nq + t, 0)),
            pl.BlockSpec((None, None, mem, D_MODEL), lambda b, t: (layer, b, 0, 0)),
            pl.BlockSpec((None, None, mem, D_MODEL), lambda b, t: (layer, b, 0, 0)),
        ],
        out_specs=pl.BlockSpec((tq, D_MODEL), lambda b, t: (b * nq + t, 0)),
        out_shape=jax.ShapeDtypeStruct((Bsz * T, D_MODEL), F32 if precise else BF16),
        compiler_params=_cparams(("parallel", "arbitrary")),
        name="xattn",
    )(q, k, v)


def _xattn_rows_kernel(q_ref, k_ref, v_ref, o_ref, *, precise):
    mm = functools.partial(_dotx, precise=precise)
    T = q_ref.shape[0]
    q4 = jnp.concatenate([q_ref[:, h * X_HD:(h + 1) * X_HD] for h in range(X_HEADS)], axis=0)
    s = mm(k_ref[...], q4, NT_DIMS) * (X_HD ** -0.5)
    row_head = jnp.bitwise_and(lax.broadcasted_iota(jnp.int32, s.shape, 0), X_HEADS - 1)
    col = lax.broadcasted_iota(jnp.int32, s.shape, 1)
    col_head = sum((col >= h * T).astype(jnp.int32) for h in range(1, X_HEADS))
    s = jnp.where(row_head == col_head, s, -jnp.inf)
    e = jnp.exp(s - jnp.max(s, axis=0, keepdims=True))
    l_row = jnp.sum(e, axis=0, keepdims=True)
    pv = mm(e, v_ref[...], TN_DIMS)
    n = X_HEADS * T
    eye = lax.broadcasted_iota(jnp.int32, (n, n), 0) == lax.broadcasted_iota(jnp.int32, (n, n), 1)
    l_col = jnp.sum(jnp.where(eye, l_row, 0.0), axis=1, keepdims=True)
    o4 = pv / l_col
    for h in range(X_HEADS):
        o_ref[:, h * X_HD:(h + 1) * X_HD] = o4[h * T:(h + 1) * T, :].astype(o_ref.dtype)


def _xattn_rows(q, k, v, *, layer, Bsz, T, precise=False):
    rows = k.shape[2]
    kv_spec = pl.BlockSpec((None, None, rows, X_HD), lambda b: (layer, b, 0, 0))
    return pl.pallas_call(
        functools.partial(_xattn_rows_kernel, precise=precise),
        grid=(Bsz,),
        in_specs=[pl.BlockSpec((T, D_MODEL), lambda b: (b, 0)), kv_spec, kv_spec],
        out_specs=pl.BlockSpec((T, D_MODEL), lambda b: (b, 0)),
        out_shape=jax.ShapeDtypeStruct((Bsz * T, D_MODEL), F32 if precise else BF16),
        compiler_params=_cparams(("parallel",)),
        name="xattn_rows",
    )(q, k, v)


def _swiglu_kernel(*refs, nf, precise):
    if precise:
        x_ref, g_ref, wg_ref, wu_ref, wd_ref, wgl_ref, wul_ref, wdl_ref, o_ref, xs_ref, acc_ref, xl_ref = refs
    else:
        x_ref, g_ref, wg_ref, wu_ref, wd_ref, o_ref, xs_ref, acc_ref = refs
        wgl_ref = wul_ref = wdl_ref = xl_ref = None
    f = pl.program_id(1)

    @pl.when(f == 0)
    def _():
        xn = _rms_rows(x_ref[...]) * g_ref[...]
        if precise:
            xs_ref[...], xl_ref[...] = _hl(xn)
        else:
            xs_ref[...] = xn.astype(BF16)
        acc_ref[...] = jnp.zeros_like(acc_ref)

    xb = xs_ref[...]
    xl = xl_ref[...] if precise else None
    a = _dot_w(xb, xl, wg_ref, wgl_ref)
    u = _dot_w(xb, xl, wu_ref, wul_ref)
    hmid = _silu(a) * u
    hh, hl = _hl(hmid) if precise else (hmid.astype(BF16), None)
    acc_ref[...] += _dot_w(hh, hl, wd_ref, wdl_ref)

    @pl.when(f == nf - 1)
    def _():
        o_ref[...] = x_ref[...] + acc_ref[...]


def _swiglu(x, gain, wg, wu, wd, *, lo=None, tm, tf):
    M, K = x.shape
    F = wg.shape[1]
    tm = min(tm, M)
    nf = F // tf
    precise = lo is not None
    up = pl.BlockSpec((K, tf), lambda i, f: (0, f))
    down = pl.BlockSpec((tf, K), lambda i, f: (f, 0))
    return pl.pallas_call(
        functools.partial(_swiglu_kernel, nf=nf, precise=precise),
        grid=(M // tm, nf),
        in_specs=[pl.BlockSpec((tm, K), lambda i, f: (i, 0)), pl.BlockSpec((1, K), lambda i, f: (0, 0)),
                  up, up, down] + ([up, up, down] if precise else []),
        out_specs=pl.BlockSpec((tm, K), lambda i, f: (i, 0)),
        out_shape=jax.ShapeDtypeStruct((M, K), F32),
        scratch_shapes=[pltpu.VMEM((tm, K), BF16), pltpu.VMEM((tm, K), F32)]
        + ([pltpu.VMEM((tm, K), BF16)] if precise else []),
        compiler_params=_cparams(("parallel", "arbitrary")),
        name="swiglu",
    )(x, gain.reshape(1, K), wg, wu, wd, *(lo if precise else ()))


def _router_kernel(x_ref, g_ref, wr_ref, un_ref, idx_ref, wt_ref):
    un = _rms_rows(x_ref[...]) * g_ref[...]
    un_ref[...] = un
    logits = _dotx(un, wr_ref[...], NN_DIMS, True)
    lane = lax.broadcasted_iota(jnp.int32, logits.shape, 1)
    lg = jnp.where(lane < N_EXPERTS, logits, -jnp.inf)
    v1 = jnp.max(lg, axis=-1, keepdims=True)
    i1 = jnp.min(jnp.where(lg == v1, lane, LANES), axis=-1, keepdims=True)
    lg2 = jnp.where(lane == i1, -jnp.inf, lg)
    v2 = jnp.max(lg2, axis=-1, keepdims=True)
    i2 = jnp.min(jnp.where(lg2 == v2, lane, LANES), axis=-1, keepdims=True)
    e2 = jnp.exp(v2 - v1)
    w1 = 1.0 / (1.0 + e2)
    w2 = e2 / (1.0 + e2)
    idx_ref[...] = jnp.where(lane == 0, i1, jnp.where(lane == 1, i2, 0))
    wt_ref[...] = jnp.where(lane == 0, w1, jnp.where(lane == 1, w2, 0.0))


def _router(x, gain, wr_pad, tm):
    M, K = x.shape
    tm = min(tm, M)
    return pl.pallas_call(
        _router_kernel,
        grid=(M // tm,),
        in_specs=[
            pl.BlockSpec((tm, K), lambda i: (i, 0)),
            pl.BlockSpec((1, K), lambda i: (0, 0)),
            pl.BlockSpec((K, LANES), lambda i: (0, 0)),
        ],
        out_specs=[
            pl.BlockSpec((tm, K), lambda i: (i, 0)),
            pl.BlockSpec((tm, LANES), lambda i: (i, 0)),
            pl.BlockSpec((tm, LANES), lambda i: (i, 0)),
        ],
        out_shape=[
            jax.ShapeDtypeStruct((M, K), F32),
            jax.ShapeDtypeStruct((M, LANES), jnp.int32),
            jax.ShapeDtypeStruct((M, LANES), F32),
        ],
        compiler_params=_cparams(("parallel",)),
        name="moe_router",
    )(x, gain.reshape(1, K), wr_pad)


def _row_copy(src_hbm, row, dst_vmem, r, sem):
    return pltpu.make_async_copy(src_hbm.at[pl.ds(row, 1), :], dst_vmem.at[pl.ds(r, 1), :], sem)


def _moe_gather_kernel(src_ref, nu_ref, xp_hbm, xs_hbm, o_ref, sem, *, n_prompt):
    i = pl.program_id(0)
    tm = o_ref.shape[0]

    @pl.when(i < nu_ref[0])
    def _():
        def issue(r, carry):
            tok = src_ref[i * tm + r]

            @pl.when(tok < n_prompt)
            def _():
                _row_copy(xp_hbm, tok, o_ref, r, sem).start()

            @pl.when(tok >= n_prompt)
            def _():
                _row_copy(xs_hbm, tok - n_prompt, o_ref, r, sem).start()
            return carry

        lax.fori_loop(0, tm, issue, 0, unroll=8)
        pltpu.make_async_copy(xp_hbm.at[pl.ds(0, tm), :], o_ref, sem).wait()

    @pl.when(i >= nu_ref[0])
    def _():
        o_ref[...] = jnp.zeros_like(o_ref)


def _moe_gather(src_tok, n_used, un_p, un_s, n_tiles):
    K = un_p.shape[1]
    return pl.pallas_call(
        functools.partial(_moe_gather_kernel, n_prompt=un_p.shape[0]),
        grid_spec=pltpu.PrefetchScalarGridSpec(
            num_scalar_prefetch=2,
            grid=(n_tiles,),
            in_specs=[pl.BlockSpec(memory_space=pl.ANY), pl.BlockSpec(memory_space=pl.ANY)],
            out_specs=pl.BlockSpec((MOE_TM, K), lambda i, src, nu: (i, 0)),
            scratch_shapes=[pltpu.SemaphoreType.DMA(())],
        ),
        out_shape=jax.ShapeDtypeStruct((n_tiles * MOE_TM, K), F32),
        compiler_params=_cparams(("arbitrary",)),
        name="moe_gather",
    )(src_tok, n_used, un_p, un_s)


def _moe_ffn_kernel(te_ref, nu_ref, x_ref, wg_ref, wu_ref, wd_ref, o_ref, xs_ref, acc_ref, *, nf):
    i = pl.program_id(0)
    f = pl.program_id(1)

    @pl.when(i < nu_ref[0])
    def _():
        @pl.when(f == 0)
        def _():
            xs_ref[...] = x_ref[...].astype(BF16)
            acc_ref[...] = jnp.zeros_like(acc_ref)

        xb = xs_ref[...]
        a = jnp.dot(xb, wg_ref[...], preferred_element_type=F32)
        u = jnp.dot(xb, wu_ref[...], preferred_element_type=F32)
        acc_ref[...] += jnp.dot((_silu(a) * u).astype(BF16), wd_ref[...], preferred_element_type=F32)

        @pl.when(f == nf - 1)
        def _():
            o_ref[...] = acc_ref[...]

    @pl.when((i >= nu_ref[0]) & (f == 0))
    def _():
        o_ref[...] = jnp.zeros_like(o_ref)


def _moe_ffn(tile_expert, n_used, xs, wg, wu, wd, n_tiles):
    K = xs.shape[1]
    nf = D_FF_E // MOE_TF

    def tile(i, nu):
        return jnp.minimum(i, nu[0] - 1)

    def fsel(i, f, nu):
        return jnp.where(i < nu[0], f, nf - 1)

    return pl.pallas_call(
        functools.partial(_moe_ffn_kernel, nf=nf),
        grid_spec=pltpu.PrefetchScalarGridSpec(
            num_scalar_prefetch=2,
            grid=(n_tiles, nf),
            in_specs=[
                pl.BlockSpec((MOE_TM, K), lambda i, f, te, nu: (tile(i, nu), 0)),
                pl.BlockSpec((None, K, MOE_TF), lambda i, f, te, nu: (te[tile(i, nu)], 0, fsel(i, f, nu))),
                pl.BlockSpec((None, K, MOE_TF), lambda i, f, te, nu: (te[tile(i, nu)], 0, fsel(i, f, nu))),
                pl.BlockSpec((None, MOE_TF, K), lambda i, f, te, nu: (te[tile(i, nu)], fsel(i, f, nu), 0)),
            ],
            out_specs=pl.BlockSpec((MOE_TM, K), lambda i, f, te, nu: (i, 0)),
            scratch_shapes=[pltpu.VMEM((MOE_TM, K), BF16), pltpu.VMEM((MOE_TM, K), F32)],
        ),
        out_shape=jax.ShapeDtypeStruct((n_tiles * MOE_TM, K), F32),
        compiler_params=_cparams(("arbitrary", "arbitrary")),
        name="moe_ffn",
    )(tile_expert, n_used, xs, wg, wu, wd)


def _moe_combine_kernel(dest_ref, x_ref, wt_ref, ys_hbm, o_ref, buf, sem, *, row0):
    i = pl.program_id(0)
    tc = x_ref.shape[0]

    def issue(r, carry):
        t = row0 + i * tc + r
        _row_copy(ys_hbm, dest_ref[2 * t], buf.at[0], r, sem).start()
        _row_copy(ys_hbm, dest_ref[2 * t + 1], buf.at[1], r, sem).start()
        return carry

    lax.fori_loop(0, tc, issue, 0, unroll=8)
    pltpu.make_async_copy(ys_hbm.at[pl.ds(0, tc), :], buf.at[0], sem).wait()
    pltpu.make_async_copy(ys_hbm.at[pl.ds(0, tc), :], buf.at[1], sem).wait()
    wt = wt_ref[...]
    o_ref[...] = x_ref[...] + (wt[:, 0:1] * buf[0] + wt[:, 1:2] * buf[1])


def _moe_combine(dest, x, wts, ys, *, row0, tc):
    M, K = x.shape
    tc = min(tc, M)
    return pl.pallas_call(
        functools.partial(_moe_combine_kernel, row0=row0),
        grid_spec=pltpu.PrefetchScalarGridSpec(
            num_scalar_prefetch=1,
            grid=(M // tc,),
            in_specs=[
                pl.BlockSpec((tc, K), lambda i, d: (i, 0)),
                pl.BlockSpec((tc, LANES), lambda i, d: (i, 0)),
                pl.BlockSpec(memory_space=pl.ANY),
            ],
            out_specs=pl.BlockSpec((tc, K), lambda i, d: (i, 0)),
            scratch_shapes=[pltpu.VMEM((2, tc, K), F32), pltpu.SemaphoreType.DMA(())],
        ),
        out_shape=jax.ShapeDtypeStruct((M, K), F32),
        compiler_params=_cparams(("arbitrary",)),
        name="moe_combine",
    )(dest, x, wts, ys)


def _moe_plan(idx_all, n_tiles):
    N = idx_all.shape[0]
    sel = (idx_all[:, :, None] == jnp.arange(N_EXPERTS, dtype=jnp.int32)).astype(jnp.int32)
    per_tok = jnp.sum(sel, axis=1)
    before = jnp.cumsum(per_tok, axis=0) - per_tok
    counts = jnp.sum(per_tok, axis=0)
    padded = ((counts + MOE_TM - 1) // MOE_TM) * MOE_TM
    g_end = jnp.cumsum(padded)
    g_start = g_end - padded
    dest = jnp.sum(sel * (g_start + before)[:, None, :], axis=2)
    tok = jnp.broadcast_to(jnp.arange(N, dtype=jnp.int32)[:, None], (N, 2))
    src_tok = jnp.zeros((n_tiles * MOE_TM,), jnp.int32).at[dest.reshape(-1)].set(tok.reshape(-1))
    starts = jnp.arange(n_tiles, dtype=jnp.int32) * MOE_TM
    tile_expert = jnp.minimum(jnp.sum((starts[:, None] >= g_end[None, :]).astype(jnp.int32), axis=1),
                              N_EXPERTS - 1).astype(jnp.int32)
    n_used = (g_end[-1:] // MOE_TM).astype(jnp.int32)
    return dest.reshape(-1).astype(jnp.int32), src_tok, tile_expert, n_used


def _moe_layer(xp, xs, gain, wr_pad, wg, wu, wd):
    Np, Ns = xp.shape[0], xs.shape[0]
    un_p, idx_p, wt_p = _router(xp, gain, wr_pad, 1024)
    un_s, idx_s, wt_s = _router(xs, gain, wr_pad, 1024)
    idx_all = jnp.concatenate([idx_p[:, :2], idx_s[:, :2]], axis=0)
    n_tiles = (2 * (Np + Ns)) // MOE_TM + N_EXPERTS
    dest, src_tok, tile_expert, n_used = _moe_plan(idx_all, n_tiles)
    rows = _moe_gather(src_tok, n_used, un_p, un_s, n_tiles)
    ys = _moe_ffn(tile_expert, n_used, rows, wg, wu, wd, n_tiles)
    xp = _moe_combine(dest, xp, wt_p, ys, row0=0, tc=256)
    xs = _moe_combine(dest, xs, wt_s, ys, row0=Np, tc=256)
    return xp, xs


def _pad_cols(w, width):
    return jnp.pad(w, ((0, 0), (0, width - w.shape[1])))


def _tail8(buf):
    return jnp.pad(buf, ((0, 0), (SUBLANES - (CONV_W - 1), 0), (0, 0)))


def kernel(x_prompt, x_sample, mem_prompt, state_mlstm_C, state_mlstm_n, state_mlstm_m, state_mlstm_conv,
           state_ssd_h, state_ssd_conv, cache_mem_k, cache_mem_v, norm_mix, norm_xattn, norm_mem, norm_ffn,
           norm_final, a_w_in, a_conv_w, a_conv_b, a_b_i, a_b_f, a_norm, a_w_out, b_w_in, b_conv_w, b_conv_b,
           b_dt_bias, b_A_log, b_D, b_norm, b_w_out, x_wq, x_wk, x_wv, x_wo, f_w_gate, f_w_up, f_w_down,
           e_router, e_w_gate, e_w_up, e_w_down):
    Bp, Tp, D = x_prompt.shape
    Bs, Ts, _ = x_sample.shape
    Tpad = SAMPLE_PAD_T
    Np, Ns = Bp * Tp, Bs * Tpad
    mem_len = mem_prompt.shape[1]
    GN2 = 2 * B_GROUPS * B_STATE

    groups = {
        "p": dict(B=Bp, T=Tp, L=SCAN_L, tv=SCAN_L, tm=1024, tq=512),
        "s": dict(B=Bs, T=Tpad, L=Tpad, tv=Ts, tm=1024, tq=Tpad),
    }
    xres = {
        "p": x_prompt.reshape(Np, D),
        "s": jnp.pad(x_sample, ((0, 0), (0, Tpad - Ts), (0, 0))).reshape(Ns, D),
    }
    mem_kv = {"s": (cache_mem_k.reshape(DEPTH, Bs, mem_len * X_HEADS, X_HD),
                    cache_mem_v.reshape(DEPTH, Bs, mem_len * X_HEADS, X_HD))}
    mem_flat = mem_prompt.reshape(Bp * mem_len, D)

    st = {
        "p": dict(
            C=jnp.zeros((2, Bp, A_HEADS, A_DV, A_DQK), F32), n=jnp.zeros((2, Bp, A_HEADS, A_DQK), F32),
            m=jnp.full((2, Bp, A_HEADS), M_INIT, F32), aconv=jnp.zeros((2, Bp, CONV_W - 1, 2 * A_QK), F32),
            h=jnp.zeros((2, Bp, B_HEADS, B_HEADDIM, B_STATE), F32),
            bconv=jnp.zeros((2, Bp, CONV_W - 1, B_INNER + GN2), F32)),
        "s": dict(C=state_mlstm_C, n=state_mlstm_n, m=state_mlstm_m, aconv=state_mlstm_conv,
                  h=state_ssd_h, bconv=state_ssd_conv),
    }
    new = {k: dict(Cnm=None, h=None, aconv=[], bconv=[]) for k in groups}
    pk, pv = [], []

    def mmw(k, hp, x, w, **kw):
        if hp:
            hi, lo = _split_bf16(w)
            return _mm(x, hi, w_lo=lo, **dict(kw, tm=PRECISE_TM))
        return _mm(x, w.astype(BF16), **kw)

    for i in range(DEPTH):
        j = i // 2
        hp = i < PRECISE_LAYERS
        if i % 2 == 0:
            w_in = _pad_cols(a_w_in[j], A_PROJ_W)
            gate_b = _pad_cols(jnp.concatenate([a_b_i[j], a_b_f[j]])[None, :], LANES)
            for k, cfg in groups.items():
                B, T = cfg["B"], cfg["T"]
                proj = mmw(k, hp, xres[k], w_in, gain=norm_mix[i], tm=cfg["tm"], tn=1280, name="a_proj")
                s0 = st[k]
                hn, C, n, m = _mlstm(
                    proj, a_conv_w[j], a_conv_b[j][None, :], gate_b, a_norm[j][None, :], _tail8(s0["aconv"][j]),
                    s0["C"], s0["n"], jnp.broadcast_to(s0["m"][..., None], s0["m"].shape + (LANES,)),
                    layer=j, Bsz=B, T=T, L=cfg["L"], t_valid=cfg["tv"], precise=hp, prev=new[k]["Cnm"])
                new[k]["Cnm"] = (C, n, m)
                xres[k] = mmw(k, hp, hn, a_w_out[j], res=xres[k], tm=cfg["tm"], tn=1024, name="a_out")
                tv = T if k == "p" else Ts
                raw = proj.reshape(B, T, A_PROJ_W)[:, tv - (CONV_W - 1):tv, :2 * A_QK]
                new[k]["aconv"].append(raw)
        else:
            w_in = _pad_cols(b_w_in[j], B_PROJ_W)
            cw, cbias = b_conv_w[j], b_conv_b[j][None, :]
            dtb = _pad_cols(b_dt_bias[j][None, :], LANES)
            alog = _pad_cols(b_A_log[j][None, :], LANES)
            dskip = jnp.repeat(b_D[j], B_HEADDIM)[None, :]
            for k, cfg in groups.items():
                B, T = cfg["B"], cfg["T"]
                proj = mmw(k, hp, xres[k], w_in, gain=norm_mix[i], tm=cfg["tm"], tn=1792, name="b_proj")
                s0 = st[k]
                buf8 = _tail8(s0["bconv"][j])
                yz, h = _ssd(
                    proj, cw[:, :B_INNER], cbias[:, :B_INNER], cw[:, B_INNER:], cbias[:, B_INNER:], dtb, alog,
                    dskip, b_norm[j][None, :], buf8[..., :B_INNER], buf8[..., B_INNER:],
                    s0["h"].reshape(-1, B, B_INNER, B_STATE), layer=j, Bsz=B, T=T, L=cfg["L"],
                    t_valid=cfg["tv"], precise=hp, prev=new[k]["h"])
                new[k]["h"] = (h,)
                xres[k] = mmw(k, hp, yz, b_w_out[j], res=xres[k], tm=cfg["tm"], tn=1024, name="b_out")
                tv = T if k == "p" else Ts
                raw = proj.reshape(B, T, B_PROJ_W)[:, tv - (CONV_W - 1):tv, B_INNER:2 * B_INNER + GN2]
                new[k]["bconv"].append(raw)

        kp = mmw("p", hp, mem_flat, x_wk[i], gain=norm_mem[i], tm=1024, tn=1024, name="mem_k")
        vp = mmw("p", hp, mem_flat, x_wv[i], gain=norm_mem[i], tm=1024, tn=1024, name="mem_v")
        pk.append(kp); pv.append(vp)
        for k, cfg in groups.items():
            B, T = cfg["B"], cfg["T"]
            pr = hp
            q = mmw(k, hp, xres[k], x_wq[i], gain=norm_xattn[i], tm=cfg["tm"], tn=1024,
                    out_dtype=F32 if pr else BF16, name="x_q")
            if k == "p":
                o = _xattn(q, kp.reshape(1, B, mem_len, D), vp.reshape(1, B, mem_len, D), layer=0, Bsz=B, T=T,
                           tq=cfg["tq"], precise=pr)
            else:
                o = _xattn_rows(q, mem_kv["s"][0], mem_kv["s"][1], layer=i, Bsz=B, T=T, precise=pr)
            xres[k] = mmw(k, hp, o, x_wo[i], res=xres[k], tm=cfg["tm"], tn=1024, name="x_o")

        if i % 2 == 0:
            ws = (f_w_gate[j], f_w_up[j], f_w_down[j])
            for k, cfg in groups.items():
                if hp:
                    parts = [_split_bf16(w) for w in ws]
                    xres[k] = _swiglu(xres[k], norm_ffn[i], *[p[0] for p in parts], lo=tuple(p[1] for p in parts),
                                      tm=512, tf=256)
                else:
                    xres[k] = _swiglu(xres[k], norm_ffn[i], *[w.astype(BF16) for w in ws], tm=512, tf=1408)
        else:
            xres["p"], xres["s"] = _moe_layer(
                xres["p"], xres["s"], norm_ffn[i], _pad_cols(e_router[j], LANES),
                e_w_gate[j].astype(BF16), e_w_up[j].astype(BF16), e_w_down[j].astype(BF16))

    y_p = _final_norm(xres["p"], norm_final, 1024).reshape(Bp, Tp, D)
    y_s = _final_norm(xres["s"], norm_final, 1024).reshape(Bs, Tpad, D)[:, :Ts]

    def states(k):
        C, n, m = new[k]["Cnm"]
        h = new[k]["h"][0]
        return (C, n, m[..., 0], jnp.stack(new[k]["aconv"]),
                h.reshape(h.shape[0], h.shape[1], B_HEADS, B_HEADDIM, B_STATE), jnp.stack(new[k]["bconv"]))

    p_mem_k = jnp.stack(pk).reshape(DEPTH, Bp, mem_len, X_HEADS, X_HD)
    p_mem_v = jnp.stack(pv).reshape(DEPTH, Bp, mem_len, X_HEADS, X_HD)
    return (y_p, y_s) + states("p") + (p_mem_k, p_mem_v) + states("s")
```

```python
import functools

import jax
import jax.numpy as jnp
from jax import lax
from jax.experimental import pallas as pl
from jax.experimental.pallas import tpu as pltpu

F32 = jnp.float32
BF16 = jnp.bfloat16

D_MODEL = 1024
DEPTH = 4
CONV_W = 4
EPS = 1e-6
M_INIT = -1e30
A_HEADS = 4
A_DQK = 256
A_DV = 512
A_QK = A_HEADS * A_DQK
A_VD = A_HEADS * A_DV
B_INNER = 2 * D_MODEL
B_HEADDIM = 64
B_HEADS = B_INNER // B_HEADDIM
B_GROUPS = 4
B_HPG = B_HEADS // B_GROUPS
B_STATE = 128
B_GW = B_HPG * B_HEADDIM
X_HEADS = 4
X_HD = D_MODEL // X_HEADS
N_EXPERTS = 8
D_FF_E = 3584

LANES = 128
SUBLANES = 8
VMEM_LIMIT_BYTES = 56 * 1024 * 1024

A_PROJ_W = 2 * A_QK + 2 * A_VD + 256
B_PROJ_W = 2 * B_INNER + 2 * B_GROUPS * B_STATE + 256
A_GATE_BLK = (2 * A_QK + 2 * A_VD) // LANES
B_DT_BLK = (2 * B_INNER + 2 * B_GROUPS * B_STATE) // LANES

SAMPLE_PAD_T = 8
MOE_TM = 512
MOE_TF = 1792
SCAN_L = 256
PRECISE_LAYERS = 2
PRECISE_TM = 512


def _cparams(sem):
    return pltpu.CompilerParams(dimension_semantics=sem, vmem_limit_bytes=VMEM_LIMIT_BYTES)


def _silu(x):
    return x * jax.nn.sigmoid(x)


def _softplus(x):
    return jnp.maximum(x, 0.0) + jnp.log1p(jnp.exp(-jnp.abs(x)))


def _rms_rows(x):
    return x * lax.rsqrt(jnp.mean(x * x, axis=-1, keepdims=True) + EPS)


def _hl(x):
    hi = x.astype(BF16)
    return hi, (x - hi.astype(F32)).astype(BF16)


def _dotx(a, b, dims, precise):
    d = lambda u, v: lax.dot_general(u, v, dims, preferred_element_type=F32)
    if not precise:
        return d(a.astype(BF16), b.astype(BF16))
    ah, al = _hl(a.astype(F32))
    bh, bl = _hl(b.astype(F32))
    return d(ah, bh) + (d(al, bh) + d(ah, bl))


def _dot_w(xh, xl, w_ref, wl_ref):
    acc = jnp.dot(xh, w_ref[...], preferred_element_type=F32)
    if wl_ref is not None:
        acc = acc + (jnp.dot(xl, w_ref[...], preferred_element_type=F32)
                     + jnp.dot(xh, wl_ref[...], preferred_element_type=F32))
    return acc


def _mm_kernel(*refs, has_gain, has_res, stage, precise):
    it = iter(refs)
    x_ref = next(it)
    w_ref = next(it)
    wl_ref = next(it) if precise else None
    g_ref = next(it) if has_gain else None
    r_ref = next(it) if has_res else None
    o_ref = next(it)
    xs_ref = next(it) if stage else None
    xl_ref = next(it) if precise else None

    if stage:
        @pl.when(pl.program_id(1) == 0)
        def _():
            x = x_ref[...].astype(F32)
            if has_gain:
                x = _rms_rows(x) * g_ref[...]
            if precise:
                xs_ref[...], xl_ref[...] = _hl(x)
            else:
                xs_ref[...] = x.astype(BF16)
        lhs = xs_ref[...]
    else:
        lhs = x_ref[...]
    acc = _dot_w(lhs, xl_ref[...] if precise else None, w_ref, wl_ref)
    if has_res:
        acc = acc + r_ref[...]
    o_ref[...] = acc.astype(o_ref.dtype)


def _mm(x, w, *, w_lo=None, gain=None, res=None, tm, tn, out_dtype=F32, name="mm"):
    M, K = x.shape
    N = w.shape[1]
    tm = min(tm, M)
    assert M % tm == 0 and N % tn == 0, (M, tm, N, tn)
    precise = w_lo is not None
    stage = precise or gain is not None or x.dtype != BF16
    wspec = pl.BlockSpec((K, tn), lambda i, j: (0, j))
    ins = [x, w]
    specs = [pl.BlockSpec((tm, K), lambda i, j: (i, 0)), wspec]
    if precise:
        ins.append(w_lo)
        specs.append(wspec)
    if gain is not None:
        ins.append(gain.reshape(1, K).astype(F32))
        specs.append(pl.BlockSpec((1, K), lambda i, j: (0, 0)))
    if res is not None:
        ins.append(res)
        specs.append(pl.BlockSpec((tm, tn), lambda i, j: (i, j)))
    scratch = [pltpu.VMEM((tm, K), BF16)] * (2 if precise else 1) if stage else []
    return pl.pallas_call(
        functools.partial(_mm_kernel, has_gain=gain is not None, has_res=res is not None, stage=stage,
                          precise=precise),
        grid=(M // tm, N // tn),
        in_specs=specs,
        out_specs=pl.BlockSpec((tm, tn), lambda i, j: (i, j)),
        out_shape=jax.ShapeDtypeStruct((M, N), out_dtype),
        scratch_shapes=scratch,
        compiler_params=_cparams(("parallel", "arbitrary")),
        name=name,
    )(*ins)


def _split_bf16(w):
    bits = lax.bitcast_convert_type(w, jnp.uint32)
    hi = lax.bitcast_convert_type(bits & jnp.uint32(0xFFFF0000), F32)
    return hi.astype(BF16), (w - hi).astype(BF16)


def _norm_kernel(x_ref, g_ref, o_ref):
    o_ref[...] = _rms_rows(x_ref[...]) * g_ref[...]


def _final_norm(x, g, tm):
    M, K = x.shape
    tm = min(tm, M)
    return pl.pallas_call(
        _norm_kernel,
        grid=(M // tm,),
        in_specs=[pl.BlockSpec((tm, K), lambda i: (i, 0)), pl.BlockSpec((1, K), lambda i: (0, 0))],
        out_specs=pl.BlockSpec((tm, K), lambda i: (i, 0)),
        out_shape=jax.ShapeDtypeStruct((M, K), F32),
        compiler_params=_cparams(("parallel",)),
        name="final_norm",
    )(x, g.reshape(1, K))


def _conv_silu(x, tail_ref, w_ref, b_ref):
    L = x.shape[0]
    xp = jnp.concatenate([tail_ref[...], x], axis=0)
    y = b_ref[...] + w_ref[CONV_W - 1:CONV_W, :] * x
    for j in range(CONV_W - 1):
        shifted = pltpu.roll(xp, CONV_W - 1 - j, axis=0)[SUBLANES:SUBLANES + L]
        y = y + w_ref[j:j + 1, :] * shifted
    tail_ref[...] = x[L - SUBLANES:L]
    return _silu(y)


def _cumsum_rows(x):
    L = x.shape[0]
    row = lax.broadcasted_iota(jnp.int32, x.shape, 0)
    d = 1
    while d < L:
        x = x + jnp.where(row >= d, pltpu.roll(x, d, axis=0), 0.0)
        d *= 2
    return x


def _col_to_row(col, eye):
    return jnp.sum(jnp.where(eye, col, 0.0), axis=0, keepdims=True)


def _store_state(own_slot, out_refs, state_refs):
    for o_ref, s_ref in zip(out_refs, state_refs):
        if own_slot is None:
            o_ref[...] = s_ref[...]
        else:
            for l in range(o_ref.shape[0]):
                o_ref[l] = s_ref[...] if l == own_slot else jnp.zeros(s_ref.shape, s_ref.dtype)


def _state_specs(shapes, n_layers, layer, Bsz, prev):
    def spec(shape):
        zeros = (0,) * len(shape)
        if prev is None:
            return pl.BlockSpec((n_layers, None) + shape, lambda b, c: (0, b) + zeros)
        return pl.BlockSpec((None, None) + shape, lambda b, c: (layer, b) + zeros)
    return ([spec(s) for s in shapes],
            [jax.ShapeDtypeStruct((n_layers, Bsz) + s, F32) for s in shapes])


NN_DIMS = (((1,), (0,)), ((), ()))
NT_DIMS = (((1,), (1,)), ((), ()))
TN_DIMS = (((0,), (0,)), ((), ()))


def _mlstm_kernel(qk_ref, v_ref, o_ref, gt_ref, cw_ref, cb_ref, gb_ref, gout_ref, buf_ref,
                  C0_ref, n0_ref, m0_ref, *rest, nc, t_valid, precise, own_slot):
    hn_ref, Cout_ref, nout_ref, mout_ref, C_s, n_s, m_s, tail_s = rest[-8:]
    L = qk_ref.shape[0]
    c = pl.program_id(1)
    mm = functools.partial(_dotx, precise=precise)

    @pl.when(c == 0)
    def _():
        C_s[...] = C0_ref[...]
        n_s[...] = n0_ref[...]
        m_s[...] = m0_ref[...]
        tail_s[...] = buf_ref[...]

    qk = _conv_silu(qk_ref[...], tail_s, cw_ref, cb_ref)

    g = gt_ref[...] + gb_ref[...]
    lane = lax.broadcasted_iota(jnp.int32, g.shape, 1)
    logf = jnp.minimum(g, 0.0) - jnp.log1p(jnp.exp(-jnp.abs(g)))
    if t_valid < L:
        valid = lax.broadcasted_iota(jnp.int32, g.shape, 0) < t_valid
        logf = jnp.where(valid, logf, 0.0)
        g = jnp.where(valid, g, -jnp.inf)
    cum = _cumsum_rows(jnp.where(lane >= A_HEADS, logf, 0.0))

    ri = lax.broadcasted_iota(jnp.int32, (L, L), 0)
    ci = lax.broadcasted_iota(jnp.int32, (L, L), 1)
    eye = ri == ci
    causal = ri >= ci

    for h in range(A_HEADS):
        qf = qk[:, h * A_DQK:(h + 1) * A_DQK]
        kf = qk[:, A_QK + h * A_DQK:A_QK + (h + 1) * A_DQK] * (A_DQK ** -0.5)
        vf = v_ref[:, h * A_DV:(h + 1) * A_DV]
        igc = g[:, h:h + 1]
        cumc = cum[:, A_HEADS + h:A_HEADS + h + 1]
        m_prev = m_s[h:h + 1, 0:1]
        bcol = igc - cumc
        brow = _col_to_row(bcol, eye)
        log_intra = jnp.where(causal, cumc + brow, -jnp.inf)
        log_inter = cumc + m_prev
        m_t = jnp.maximum(log_inter, jnp.max(log_intra, axis=-1, keepdims=True))
        w_intra = jnp.exp(log_intra - m_t)
        w_inter = jnp.exp(log_inter - m_t)
        s = mm(qf, kf, NT_DIMS) * w_intra
        C_h = C_s[h]
        num = mm(s, vf, NN_DIMS) + mm(qf, C_h, NT_DIMS) * w_inter
        n_h = n_s[h:h + 1, :]
        den = jnp.sum(s, axis=-1, keepdims=True) + w_inter * jnp.sum(qf * n_h, axis=-1, keepdims=True)
        den = jnp.maximum(jnp.abs(den), jnp.exp(-m_t))
        hh = jax.nn.sigmoid(o_ref[:, h * A_DV:(h + 1) * A_DV]) * (num / den)
        hn_ref[:, h * A_DV:(h + 1) * A_DV] = (
            _rms_rows(hh) * gout_ref[:, h * A_DV:(h + 1) * A_DV]).astype(hn_ref.dtype)

        cum_l = cumc[L - 1:L, :]
        m_l = m_t[L - 1:L, :]
        w_last = jnp.exp(bcol + (cum_l - m_l))
        decay = jnp.exp(cum_l + m_prev - m_l)
        kw = w_last * kf
        upd = mm(vf, kw, TN_DIMS)
        C_s[h] = decay * C_h + upd
        n_s[h:h + 1, :] = decay * n_h + jnp.sum(kw, axis=0, keepdims=True)
        m_s[h:h + 1, :] = jnp.broadcast_to(m_l, (1, LANES))

    @pl.when(c == nc - 1)
    def _():
        _store_state(own_slot, (Cout_ref, nout_ref, mout_ref), (C_s, n_s, m_s))


def _mlstm(proj, conv_w, conv_b, gate_b, g_out, buf8, C0, n0, m0, *, layer, Bsz, T, L, t_valid, precise=False,
           prev=None):
    nc = T // L
    st_specs, st_shapes = _state_specs([(A_HEADS, A_DV, A_DQK), (A_HEADS, A_DQK), (A_HEADS, LANES)],
                                       C0.shape[0], layer, Bsz, prev)
    n_in = 12
    row = lambda b, c: b * nc + c
    per_b3 = lambda b, c: (b, 0, 0)
    lay_b5 = lambda b, c: (layer, b, 0, 0, 0)
    lay_b4 = lambda b, c: (layer, b, 0, 0)
    const2 = lambda b, c: (0, 0)
    return pl.pallas_call(
        functools.partial(_mlstm_kernel, nc=nc, t_valid=t_valid, precise=precise,
                          own_slot=layer if prev is None else None),
        grid=(Bsz, nc),
        in_specs=[
            pl.BlockSpec((L, 2 * A_QK), lambda b, c: (row(b, c), 0)),
            pl.BlockSpec((L, A_VD), lambda b, c: (row(b, c), 1)),
            pl.BlockSpec((L, A_VD), lambda b, c: (row(b, c), 2)),
            pl.BlockSpec((L, LANES), lambda b, c: (row(b, c), A_GATE_BLK)),
            pl.BlockSpec((CONV_W, 2 * A_QK), const2),
            pl.BlockSpec((1, 2 * A_QK), const2),
            pl.BlockSpec((1, LANES), const2),
            pl.BlockSpec((1, A_VD), const2),
            pl.BlockSpec((None, SUBLANES, 2 * A_QK), per_b3),
            pl.BlockSpec((None, None, A_HEADS, A_DV, A_DQK), lay_b5),
            pl.BlockSpec((None, None, A_HEADS, A_DQK), lay_b4),
            pl.BlockSpec((None, None, A_HEADS, LANES), lay_b4),
        ] + [pl.BlockSpec(memory_space=pl.ANY)] * (0 if prev is None else 3),
        out_specs=[pl.BlockSpec((L, A_VD), lambda b, c: (row(b, c), 0))] + st_specs,
        out_shape=[jax.ShapeDtypeStruct((Bsz * T, A_VD), F32 if precise else BF16)] + st_shapes,
        input_output_aliases={} if prev is None else {n_in: 1, n_in + 1: 2, n_in + 2: 3},
        scratch_shapes=[
            pltpu.VMEM((A_HEADS, A_DV, A_DQK), F32),
            pltpu.VMEM((A_HEADS, A_DQK), F32),
            pltpu.VMEM((A_HEADS, LANES), F32),
            pltpu.VMEM((SUBLANES, 2 * A_QK), F32),
        ],
        compiler_params=_cparams(("parallel", "arbitrary")),
        name="mlstm_scan",
    )(proj, proj, proj, proj, conv_w, conv_b, gate_b, g_out, buf8, C0, n0, m0, *(prev or ()))


def _ssd_kernel(z_ref, x_ref, bc_ref, dt_ref, cwx_ref, cbx_ref, cwbc_ref, cbbc_ref, dtb_ref, alog_ref,
                dskip_ref, gn_ref, bufx_ref, bufbc_ref, h0_ref, *rest, nc, t_valid, precise, own_slot):
    y_ref, hout_ref, h_s, tailx_s, tailbc_s, xw_s = rest[-6:]
    L = z_ref.shape[0]
    c = pl.program_id(1)
    mm = functools.partial(_dotx, precise=precise)

    @pl.when(c == 0)
    def _():
        h_s[...] = h0_ref[...]
        tailx_s[...] = bufx_ref[...]
        tailbc_s[...] = bufbc_ref[...]

    xs = _conv_silu(x_ref[...], tailx_s, cwx_ref, cbx_ref)
    bc = _conv_silu(bc_ref[...], tailbc_s, cwbc_ref, cbbc_ref)

    dt = _softplus(dt_ref[...] + dtb_ref[...])
    lane = lax.broadcasted_iota(jnp.int32, dt.shape, 1)
    keep = lane < B_HEADS
    if t_valid < L:
        keep = keep & (lax.broadcasted_iota(jnp.int32, dt.shape, 0) < t_valid)
    dt = jnp.where(keep, dt, 0.0)
    cum = _cumsum_rows(dt * (-jnp.exp(alog_ref[...])))
    ecum = jnp.exp(cum)
    cum_l = cum[L - 1:L, :]
    wl = jnp.exp(cum_l - cum)
    edecay = jnp.exp(cum_l)

    ri = lax.broadcasted_iota(jnp.int32, (L, L), 0)
    ci = lax.broadcasted_iota(jnp.int32, (L, L), 1)
    eye = ri == ci
    causal = ri >= ci
    lo = lax.broadcasted_iota(jnp.int32, (L, LANES), 1) < B_HEADDIM

    GN = B_GROUPS * B_STATE
    for g in range(B_GROUPS):
        Bm = bc[:, g * B_STATE:(g + 1) * B_STATE]
        Cm = bc[:, GN + g * B_STATE:GN + (g + 1) * B_STATE]
        cb = mm(Cm, Bm, NT_DIMS)
        h_g = h_s[g * B_GW:(g + 1) * B_GW, :]
        inter = mm(Cm, h_g, NT_DIMS)
        sq = jnp.zeros((L, 1), F32)
        ys = []
        for j in range(B_HPG // 2):
            col0 = g * B_GW + j * LANES
            xpair = xs[:, col0:col0 + LANES]
            hd0 = g * B_HPG + 2 * j
            xdt = xpair * jnp.where(lo, dt[:, hd0:hd0 + 1], dt[:, hd0 + 1:hd0 + 2])
            y = jnp.zeros((L, LANES), F32)
            for half in range(2):
                cumc = cum[:, hd0 + half:hd0 + half + 1]
                seg = jnp.exp(jnp.where(causal, cumc - _col_to_row(cumc, eye), -jnp.inf))
                xh = jnp.where(lo if half == 0 else ~lo, xdt, 0.0)
                y = y + mm(cb * seg, xh, NN_DIMS)
            e_pair = jnp.where(lo, ecum[:, hd0:hd0 + 1], ecum[:, hd0 + 1:hd0 + 2])
            w_pair = jnp.where(lo, wl[:, hd0:hd0 + 1], wl[:, hd0 + 1:hd0 + 2])
            y = y + e_pair * inter[:, j * LANES:(j + 1) * LANES]
            y = y + dskip_ref[:, col0:col0 + LANES] * xpair
            y = y * _silu(z_ref[:, col0:col0 + LANES])
            sq = sq + jnp.sum(y * y, axis=-1, keepdims=True)
            ys.append(y)
            xw_s[:, j * LANES:(j + 1) * LANES] = (xdt * w_pair).astype(xw_s.dtype)
        scale = lax.rsqrt(sq * (1.0 / B_GW) + EPS)
        for j in range(B_HPG // 2):
            col0 = g * B_GW + j * LANES
            y_ref[:, col0:col0 + LANES] = (ys[j] * scale * gn_ref[:, col0:col0 + LANES]).astype(y_ref.dtype)
        upd = mm(xw_s[...], Bm, TN_DIMS)
        for r in range(B_HPG):
            hd = g * B_HPG + r
            r0 = g * B_GW + r * B_HEADDIM
            h_s[r0:r0 + B_HEADDIM, :] = (edecay[:, hd:hd + 1] * h_g[r * B_HEADDIM:(r + 1) * B_HEADDIM, :]
                                         + upd[r * B_HEADDIM:(r + 1) * B_HEADDIM, :])

    @pl.when(c == nc - 1)
    def _():
        _store_state(own_slot, (hout_ref,), (h_s,))


def _ssd(proj, cwx, cbx, cwbc, cbbc, dtb, alog, dskip, gnorm, bufx8, bufbc8, h0, *, layer, Bsz, T, L, t_valid,
         precise=False, prev=None):
    nc = T // L
    st_specs, st_shapes = _state_specs([(B_INNER, B_STATE)], h0.shape[0], layer, Bsz, prev)
    n_in = 15
    GN2 = 2 * B_GROUPS * B_STATE
    row = lambda b, c: b * nc + c
    per_b3 = lambda b, c: (b, 0, 0)
    const2 = lambda b, c: (0, 0)
    return pl.pallas_call(
        functools.partial(_ssd_kernel, nc=nc, t_valid=t_valid, precise=precise,
                          own_slot=layer if prev is None else None),
        grid=(Bsz, nc),
        in_specs=[
            pl.BlockSpec((L, B_INNER), lambda b, c: (row(b, c), 0)),
            pl.BlockSpec((L, B_INNER), lambda b, c: (row(b, c), 1)),
            pl.BlockSpec((L, GN2), lambda b, c: (row(b, c), 2 * B_INNER // GN2)),
            pl.BlockSpec((L, LANES), lambda b, c: (row(b, c), B_DT_BLK)),
            pl.BlockSpec((CONV_W, B_INNER), const2),
            pl.BlockSpec((1, B_INNER), const2),
            pl.BlockSpec((CONV_W, GN2), const2),
            pl.BlockSpec((1, GN2), const2),
            pl.BlockSpec((1, LANES), const2),
            pl.BlockSpec((1, LANES), const2),
            pl.BlockSpec((1, B_INNER), const2),
            pl.BlockSpec((1, B_INNER), const2),
            pl.BlockSpec((None, SUBLANES, B_INNER), per_b3),
            pl.BlockSpec((None, SUBLANES, GN2), per_b3),
            pl.BlockSpec((None, None, B_INNER, B_STATE), lambda b, c: (layer, b, 0, 0)),
        ] + [pl.BlockSpec(memory_space=pl.ANY)] * (0 if prev is None else 1),
        out_specs=[pl.BlockSpec((L, B_INNER), lambda b, c: (row(b, c), 0))] + st_specs,
        out_shape=[jax.ShapeDtypeStruct((Bsz * T, B_INNER), F32 if precise else BF16)] + st_shapes,
        input_output_aliases={} if prev is None else {n_in: 1},
        scratch_shapes=[
            pltpu.VMEM((B_INNER, B_STATE), F32),
            pltpu.VMEM((SUBLANES, B_INNER), F32),
            pltpu.VMEM((SUBLANES, GN2), F32),
            pltpu.VMEM((L, B_GW), F32 if precise else BF16),
        ],
        compiler_params=_cparams(("parallel", "arbitrary")),
        name="ssd_scan",
    )(proj, proj, proj, proj, cwx, cbx, cwbc, cbbc, dtb, alog, dskip, gnorm, bufx8, bufbc8, h0, *(prev or ()))


def _xattn_kernel(q_ref, k_ref, v_ref, o_ref, *, precise):
    mm = functools.partial(_dotx, precise=precise)
    for h in range(X_HEADS):
        sl = slice(h * X_HD, (h + 1) * X_HD)
        s = mm(q_ref[:, sl], k_ref[:, sl], NT_DIMS) * (X_HD ** -0.5)
        e = jnp.exp(s - jnp.max(s, axis=-1, keepdims=True))
        pv = mm(e, v_ref[:, sl], NN_DIMS)
        o_ref[:, sl] = (pv / jnp.sum(e, axis=-1, keepdims=True)).astype(o_ref.dtype)


def _xattn(q, k, v, *, layer, Bsz, T, tq, precise=False):
    nq = T // tq
    mem = k.shape[2]
    return pl.pallas_call(
        functools.partial(_xattn_kernel, precise=precise),
        grid=(Bsz, nq),
        in_specs=[
            pl.BlockSpec((tq, D_MODEL), lambda b, t: (b * nq + t, 0)),
            pl.BlockSpec((None, None, mem, D_MODEL), lambda b, t: (layer, b, 0, 0)),
            pl.BlockSpec((None, None, mem, D_MODEL), lambda b, t: (layer, b, 0, 0)),
        ],
        out_specs=pl.BlockSpec((tq, D_MODEL), lambda b, t: (b * nq + t, 0)),
        out_shape=jax.ShapeDtypeStruct((Bsz * T, D_MODEL), F32 if precise else BF16),
        compiler_params=_cparams(("parallel", "arbitrary")),
        name="xattn",
    )(q, k, v)


def _xattn_rows_kernel(q_ref, k_ref, v_ref, o_ref, *, precise):
    mm = functools.partial(_dotx, precise=precise)
    T = q_ref.shape[0]
    q4 = jnp.concatenate([q_ref[:, h * X_HD:(h + 1) * X_HD] for h in range(X_HEADS)], axis=0)
    s = mm(k_ref[...], q4, NT_DIMS) * (X_HD ** -0.5)
    row_head = jnp.bitwise_and(lax.broadcasted_iota(jnp.int32, s.shape, 0), X_HEADS - 1)
    col = lax.broadcasted_iota(jnp.int32, s.shape, 1)
    col_head = sum((col >= h * T).astype(jnp.int32) for h in range(1, X_HEADS))
    s = jnp.where(row_head == col_head, s, -jnp.inf)
    e = jnp.exp(s - jnp.max(s, axis=0, keepdims=True))
    l_row = jnp.sum(e, axis=0, keepdims=True)
    pv = mm(e, v_ref[...], TN_DIMS)
    n = X_HEADS * T
    eye = lax.broadcasted_iota(jnp.int32, (n, n), 0) == lax.broadcasted_iota(jnp.int32, (n, n), 1)
    l_col = jnp.sum(jnp.where(eye, l_row, 0.0), axis=1, keepdims=True)
    o4 = pv / l_col
    for h in range(X_HEADS):
        o_ref[:, h * X_HD:(h + 1) * X_HD] = o4[h * T:(h + 1) * T, :].astype(o_ref.dtype)


def _xattn_rows(q, k, v, *, layer, Bsz, T, precise=False):
    rows = k.shape[2]
    kv_spec = pl.BlockSpec((None, None, rows, X_HD), lambda b: (layer, b, 0, 0))
    return pl.pallas_call(
        functools.partial(_xattn_rows_kernel, precise=precise),
        grid=(Bsz,),
        in_specs=[pl.BlockSpec((T, D_MODEL), lambda b: (b, 0)), kv_spec, kv_spec],
        out_specs=pl.BlockSpec((T, D_MODEL), lambda b: (b, 0)),
        out_shape=jax.ShapeDtypeStruct((Bsz * T, D_MODEL), F32 if precise else BF16),
        compiler_params=_cparams(("parallel",)),
        name="xattn_rows",
    )(q, k, v)


def _swiglu_kernel(*refs, nf, precise):
    if precise:
        x_ref, g_ref, wg_ref, wu_ref, wd_ref, wgl_ref, wul_ref, wdl_ref, o_ref, xs_ref, acc_ref, xl_ref = refs
    else:
        x_ref, g_ref, wg_ref, wu_ref, wd_ref, o_ref, xs_ref, acc_ref = refs
        wgl_ref = wul_ref = wdl_ref = xl_ref = None
    f = pl.program_id(1)

    @pl.when(f == 0)
    def _():
        xn = _rms_rows(x_ref[...]) * g_ref[...]
        if precise:
            xs_ref[...], xl_ref[...] = _hl(xn)
        else:
            xs_ref[...] = xn.astype(BF16)
        acc_ref[...] = jnp.zeros_like(acc_ref)

    xb = xs_ref[...]
    xl = xl_ref[...] if precise else None
    a = _dot_w(xb, xl, wg_ref, wgl_ref)
    u = _dot_w(xb, xl, wu_ref, wul_ref)
    hmid = _silu(a) * u
    hh, hl = _hl(hmid) if precise else (hmid.astype(BF16), None)
    acc_ref[...] += _dot_w(hh, hl, wd_ref, wdl_ref)

    @pl.when(f == nf - 1)
    def _():
        o_ref[...] = x_ref[...] + acc_ref[...]


def _swiglu(x, gain, wg, wu, wd, *, lo=None, tm, tf):
    M, K = x.shape
    F = wg.shape[1]
    tm = min(tm, M)
    nf = F // tf
    precise = lo is not None
    up = pl.BlockSpec((K, tf), lambda i, f: (0, f))
    down = pl.BlockSpec((tf, K), lambda i, f: (f, 0))
    return pl.pallas_call(
        functools.partial(_swiglu_kernel, nf=nf, precise=precise),
        grid=(M // tm, nf),
        in_specs=[pl.BlockSpec((tm, K), lambda i, f: (i, 0)), pl.BlockSpec((1, K), lambda i, f: (0, 0)),
                  up, up, down] + ([up, up, down] if precise else []),
        out_specs=pl.BlockSpec((tm, K), lambda i, f: (i, 0)),
        out_shape=jax.ShapeDtypeStruct((M, K), F32),
        scratch_shapes=[pltpu.VMEM((tm, K), BF16), pltpu.VMEM((tm, K), F32)]
        + ([pltpu.VMEM((tm, K), BF16)] if precise else []),
        compiler_params=_cparams(("parallel", "arbitrary")),
        name="swiglu",
    )(x, gain.reshape(1, K), wg, wu, wd, *(lo if precise else ()))


def _router_kernel(x_ref, g_ref, wr_ref, un_ref, idx_ref, wt_ref):
    un = _rms_rows(x_ref[...]) * g_ref[...]
    un_ref[...] = un
    logits = _dotx(un, wr_ref[...], NN_DIMS, True)
    lane = lax.broadcasted_iota(jnp.int32, logits.shape, 1)
    lg = jnp.where(lane < N_EXPERTS, logits, -jnp.inf)
    v1 = jnp.max(lg, axis=-1, keepdims=True)
    i1 = jnp.min(jnp.where(lg == v1, lane, LANES), axis=-1, keepdims=True)
    lg2 = jnp.where(lane == i1, -jnp.inf, lg)
    v2 = jnp.max(lg2, axis=-1, keepdims=True)
    i2 = jnp.min(jnp.where(lg2 == v2, lane, LANES), axis=-1, keepdims=True)
    e2 = jnp.exp(v2 - v1)
    w1 = 1.0 / (1.0 + e2)
    w2 = e2 / (1.0 + e2)
    idx_ref[...] = jnp.where(lane == 0, i1, jnp.where(lane == 1, i2, 0))
    wt_ref[...] = jnp.where(lane == 0, w1, jnp.where(lane == 1, w2, 0.0))


def _router(x, gain, wr_pad, tm):
    M, K = x.shape
    tm = min(tm, M)
    return pl.pallas_call(
        _router_kernel,
        grid=(M // tm,),
        in_specs=[
            pl.BlockSpec((tm, K), lambda i: (i, 0)),
            pl.BlockSpec((1, K), lambda i: (0, 0)),
            pl.BlockSpec((K, LANES), lambda i: (0, 0)),
        ],
        out_specs=[
            pl.BlockSpec((tm, K), lambda i: (i, 0)),
            pl.BlockSpec((tm, LANES), lambda i: (i, 0)),
            pl.BlockSpec((tm, LANES), lambda i: (i, 0)),
        ],
        out_shape=[
            jax.ShapeDtypeStruct((M, K), F32),
            jax.ShapeDtypeStruct((M, LANES), jnp.int32),
            jax.ShapeDtypeStruct((M, LANES), F32),
        ],
        compiler_params=_cparams(("parallel",)),
        name="moe_router",
    )(x, gain.reshape(1, K), wr_pad)


def _row_copy(src_hbm, row, dst_vmem, r, sem):
    return pltpu.make_async_copy(src_hbm.at[pl.ds(row, 1), :], dst_vmem.at[pl.ds(r, 1), :], sem)


def _moe_dispatch_kernel(dest_ref, x_ref, rows_in, rows_out, sem, *, row0):
    del rows_in
    i = pl.program_id(0)
    tm = x_ref.shape[0]

    def issue(r, carry):
        t = row0 + i * tm + r
        for slot in range(2):
            pltpu.make_async_copy(x_ref.at[pl.ds(r, 1), :], rows_out.at[pl.ds(dest_ref[2 * t + slot], 1), :],
                                  sem).start()
        return carry

    lax.fori_loop(0, tm, issue, 0, unroll=8)
    for slot in range(2):
        pltpu.make_async_copy(x_ref, rows_out.at[pl.ds(0, tm), :], sem).wait()


def _moe_dispatch(dest, un, rows, *, row0, tm):
    M, K = un.shape
    tm = min(tm, M)
    return pl.pallas_call(
        functools.partial(_moe_dispatch_kernel, row0=row0),
        grid_spec=pltpu.PrefetchScalarGridSpec(
            num_scalar_prefetch=1,
            grid=(M // tm,),
            in_specs=[pl.BlockSpec((tm, K), lambda i, d: (i, 0)), pl.BlockSpec(memory_space=pl.ANY)],
            out_specs=pl.BlockSpec(memory_space=pl.ANY),
            scratch_shapes=[pltpu.SemaphoreType.DMA(())],
        ),
        out_shape=jax.ShapeDtypeStruct(rows.shape, rows.dtype),
        input_output_aliases={2: 0},
        compiler_params=_cparams(("arbitrary",)),
        name="moe_dispatch",
    )(dest, un, rows)


def _moe_ffn_kernel(te_ref, nu_ref, x_ref, wg_ref, wu_ref, wd_ref, o_ref, xs_ref, acc_ref, *, nf):
    i = pl.program_id(0)
    f = pl.program_id(1)

    @pl.when(i < nu_ref[0])
    def _():
        @pl.when(f == 0)
        def _():
            xs_ref[...] = x_ref[...].astype(BF16)
            acc_ref[...] = jnp.zeros_like(acc_ref)

        xb = xs_ref[...]
        a = jnp.dot(xb, wg_ref[...], preferred_element_type=F32)
        u = jnp.dot(xb, wu_ref[...], preferred_element_type=F32)
        acc_ref[...] += jnp.dot((_silu(a) * u).astype(BF16), wd_ref[...], preferred_element_type=F32)

        @pl.when(f == nf - 1)
        def _():
            o_ref[...] = acc_ref[...]

    @pl.when((i >= nu_ref[0]) & (f == 0))
    def _():
        o_ref[...] = jnp.zeros_like(o_ref)


def _moe_ffn(tile_expert, n_used, xs, wg, wu, wd, n_tiles):
    K = xs.shape[1]
    nf = D_FF_E // MOE_TF

    def tile(i, nu):
        return jnp.minimum(i, nu[0] - 1)

    def fsel(i, f, nu):
        return jnp.where(i < nu[0], f, nf - 1)

    return pl.pallas_call(
        functools.partial(_moe_ffn_kernel, nf=nf),
        grid_spec=pltpu.PrefetchScalarGridSpec(
            num_scalar_prefetch=2,
            grid=(n_tiles, nf),
            in_specs=[
                pl.BlockSpec((MOE_TM, K), lambda i, f, te, nu: (tile(i, nu), 0)),
                pl.BlockSpec((None, K, MOE_TF), lambda i, f, te, nu: (te[tile(i, nu)], 0, fsel(i, f, nu))),
                pl.BlockSpec((None, K, MOE_TF), lambda i, f, te, nu: (te[tile(i, nu)], 0, fsel(i, f, nu))),
                pl.BlockSpec((None, MOE_TF, K), lambda i, f, te, nu: (te[tile(i, nu)], fsel(i, f, nu), 0)),
            ],
            out_specs=pl.BlockSpec((MOE_TM, K), lambda i, f, te, nu: (i, 0)),
            scratch_shapes=[pltpu.VMEM((MOE_TM, K), BF16), pltpu.VMEM((MOE_TM, K), F32)],
        ),
        out_shape=jax.ShapeDtypeStruct((n_tiles * MOE_TM, K), F32),
        compiler_params=_cparams(("arbitrary", "arbitrary")),
        name="moe_ffn",
    )(tile_expert, n_used, xs, wg, wu, wd)


def _moe_combine_kernel(dest_ref, x_ref, wt_ref, ys_hbm, o_ref, buf, sem, *, row0):
    i = pl.program_id(0)
    tc = x_ref.shape[0]

    def issue(r, carry):
        t = row0 + i * tc + r
        _row_copy(ys_hbm, dest_ref[2 * t], buf.at[0], r, sem).start()
        _row_copy(ys_hbm, dest_ref[2 * t + 1], buf.at[1], r, sem).start()
        return carry

    lax.fori_loop(0, tc, issue, 0, unroll=8)
    pltpu.make_async_copy(ys_hbm.at[pl.ds(0, tc), :], buf.at[0], sem).wait()
    pltpu.make_async_copy(ys_hbm.at[pl.ds(0, tc), :], buf.at[1], sem).wait()
    wt = wt_ref[...]
    o_ref[...] = x_ref[...] + (wt[:, 0:1] * buf[0] + wt[:, 1:2] * buf[1])


def _moe_combine(dest, x, wts, ys, *, row0, tc):
    M, K = x.shape
    tc = min(tc, M)
    return pl.pallas_call(
        functools.partial(_moe_combine_kernel, row0=row0),
        grid_spec=pltpu.PrefetchScalarGridSpec(
            num_scalar_prefetch=1,
            grid=(M // tc,),
            in_specs=[
                pl.BlockSpec((tc, K), lambda i, d: (i, 0)),
                pl.BlockSpec((tc, LANES), lambda i, d: (i, 0)),
                pl.BlockSpec(memory_space=pl.ANY),
            ],
            out_specs=pl.BlockSpec((tc, K), lambda i, d: (i, 0)),
            scratch_shapes=[pltpu.VMEM((2, tc, K), F32), pltpu.SemaphoreType.DMA(())],
        ),
        out_shape=jax.ShapeDtypeStruct((M, K), F32),
        compiler_params=_cparams(("arbitrary",)),
        name="moe_combine",
    )(dest, x, wts, ys)


def _moe_plan(idx_all, n_tiles):
    N = idx_all.shape[0]
    sel = (idx_all[:, :, None] == jnp.arange(N_EXPERTS, dtype=jnp.int32)).astype(jnp.int32)
    per_tok = jnp.sum(sel, axis=1)
    before = jnp.cumsum(per_tok, axis=0) - per_tok
    counts = jnp.sum(per_tok, axis=0)
    padded = ((counts + MOE_TM - 1) // MOE_TM) * MOE_TM
    g_end = jnp.cumsum(padded)
    g_start = g_end - padded
    dest = jnp.sum(sel * (g_start + before)[:, None, :], axis=2)
    starts = jnp.arange(n_tiles, dtype=jnp.int32) * MOE_TM
    tile_expert = jnp.minimum(jnp.sum((starts[:, None] >= g_end[None, :]).astype(jnp.int32), axis=1),
                              N_EXPERTS - 1).astype(jnp.int32)
    n_used = (g_end[-1:] // MOE_TM).astype(jnp.int32)
    return dest.reshape(-1).astype(jnp.int32), tile_expert, n_used


def _moe_layer(xp, xs, gain, wr_pad, wg, wu, wd):
    Np, Ns = xp.shape[0], xs.shape[0]
    un_p, idx_p, wt_p = _router(xp, gain, wr_pad, 1024)
    un_s, idx_s, wt_s = _router(xs, gain, wr_pad, 1024)
    idx_all = jnp.concatenate([idx_p[:, :2], idx_s[:, :2]], axis=0)
    n_tiles = (2 * (Np + Ns)) // MOE_TM + N_EXPERTS
    dest, tile_expert, n_used = _moe_plan(idx_all, n_tiles)
    rows = jnp.zeros((n_tiles * MOE_TM, xp.shape[1]), F32)
    rows = _moe_dispatch(dest, un_p, rows, row0=0, tm=MOE_TM)
    rows = _moe_dispatch(dest, un_s, rows, row0=Np, tm=MOE_TM)
    ys = _moe_ffn(tile_expert, n_used, rows, wg, wu, wd, n_tiles)
    xp = _moe_combine(dest, xp, wt_p, ys, row0=0, tc=256)
    xs = _moe_combine(dest, xs, wt_s, ys, row0=Np, tc=256)
    return xp, xs


def _pad_cols(w, width):
    return jnp.pad(w, ((0, 0), (0, width - w.shape[1])))


def _tail8(buf):
    return jnp.pad(buf, ((0, 0), (SUBLANES - (CONV_W - 1), 0), (0, 0)))


def kernel(x_prompt, x_sample, mem_prompt, state_mlstm_C, state_mlstm_n, state_mlstm_m, state_mlstm_conv,
           state_ssd_h, state_ssd_conv, cache_mem_k, cache_mem_v, norm_mix, norm_xattn, norm_mem, norm_ffn,
           norm_final, a_w_in, a_conv_w, a_conv_b, a_b_i, a_b_f, a_norm, a_w_out, b_w_in, b_conv_w, b_conv_b,
           b_dt_bias, b_A_log, b_D, b_norm, b_w_out, x_wq, x_wk, x_wv, x_wo, f_w_gate, f_w_up, f_w_down,
           e_router, e_w_gate, e_w_up, e_w_down):
    Bp, Tp, D = x_prompt.shape
    Bs, Ts, _ = x_sample.shape
    Tpad = SAMPLE_PAD_T
    Np, Ns = Bp * Tp, Bs * Tpad
    mem_len = mem_prompt.shape[1]
    GN2 = 2 * B_GROUPS * B_STATE

    groups = {
        "p": dict(B=Bp, T=Tp, L=SCAN_L, tv=SCAN_L, tm=1024, tq=512),
        "s": dict(B=Bs, T=Tpad, L=Tpad, tv=Ts, tm=1024, tq=Tpad),
    }
    xres = {
        "p": x_prompt.reshape(Np, D),
        "s": jnp.pad(x_sample, ((0, 0), (0, Tpad - Ts), (0, 0))).reshape(Ns, D),
    }
    mem_kv = {"s": (cache_mem_k.astype(BF16).reshape(DEPTH, Bs, mem_len * X_HEADS, X_HD),
                    cache_mem_v.astype(BF16).reshape(DEPTH, Bs, mem_len * X_HEADS, X_HD))}
    mem_flat = mem_prompt.reshape(Bp * mem_len, D)

    st = {
        "p": dict(
            C=jnp.zeros((2, Bp, A_HEADS, A_DV, A_DQK), F32), n=jnp.zeros((2, Bp, A_HEADS, A_DQK), F32),
            m=jnp.full((2, Bp, A_HEADS), M_INIT, F32), aconv=jnp.zeros((2, Bp, CONV_W - 1, 2 * A_QK), F32),
            h=jnp.zeros((2, Bp, B_HEADS, B_HEADDIM, B_STATE), F32),
            bconv=jnp.zeros((2, Bp, CONV_W - 1, B_INNER + GN2), F32)),
        "s": dict(C=state_mlstm_C, n=state_mlstm_n, m=state_mlstm_m, aconv=state_mlstm_conv,
                  h=state_ssd_h, bconv=state_ssd_conv),
    }
    new = {k: dict(Cnm=None, h=None, aconv=[], bconv=[]) for k in groups}
    pk, pv = [], []

    def mmw(k, hp, x, w, **kw):
        if hp:
            hi, lo = _split_bf16(w)
            return _mm(x, hi, w_lo=lo, **dict(kw, tm=PRECISE_TM))
        return _mm(x, w.astype(BF16), **kw)

    for i in range(DEPTH):
        j = i // 2
        hp = i < PRECISE_LAYERS
        if i % 2 == 0:
            w_in = _pad_cols(a_w_in[j], A_PROJ_W)
            gate_b = _pad_cols(jnp.concatenate([a_b_i[j], a_b_f[j]])[None, :], LANES)
            for k, cfg in groups.items():
                B, T = cfg["B"], cfg["T"]
                proj = mmw(k, hp, xres[k], w_in, gain=norm_mix[i], tm=cfg["tm"], tn=1280, name="a_proj")
                s0 = st[k]
                hn, C, n, m = _mlstm(
                    proj, a_conv_w[j], a_conv_b[j][None, :], gate_b, a_norm[j][None, :], _tail8(s0["aconv"][j]),
                    s0["C"], s0["n"], jnp.broadcast_to(s0["m"][..., None], s0["m"].shape + (LANES,)),
                    layer=j, Bsz=B, T=T, L=cfg["L"], t_valid=cfg["tv"], precise=hp, prev=new[k]["Cnm"])
                new[k]["Cnm"] = (C, n, m)
                xres[k] = mmw(k, hp, hn, a_w_out[j], res=xres[k], tm=cfg["tm"], tn=1024, name="a_out")
                tv = T if k == "p" else Ts
                raw = proj.reshape(B, T, A_PROJ_W)[:, tv - (CONV_W - 1):tv, :2 * A_QK]
                new[k]["aconv"].append(raw)
        else:
            w_in = _pad_cols(b_w_in[j], B_PROJ_W)
            cw, cbias = b_conv_w[j], b_conv_b[j][None, :]
            dtb = _pad_cols(b_dt_bias[j][None, :], LANES)
            alog = _pad_cols(b_A_log[j][None, :], LANES)
            dskip = jnp.repeat(b_D[j], B_HEADDIM)[None, :]
            for k, cfg in groups.items():
                B, T = cfg["B"], cfg["T"]
                proj = mmw(k, hp, xres[k], w_in, gain=norm_mix[i], tm=cfg["tm"], tn=1792, name="b_proj")
                s0 = st[k]
                buf8 = _tail8(s0["bconv"][j])
                yz, h = _ssd(
                    proj, cw[:, :B_INNER], cbias[:, :B_INNER], cw[:, B_INNER:], cbias[:, B_INNER:], dtb, alog,
                    dskip, b_norm[j][None, :], buf8[..., :B_INNER], buf8[..., B_INNER:],
                    s0["h"].reshape(-1, B, B_INNER, B_STATE), layer=j, Bsz=B, T=T, L=cfg["L"],
                    t_valid=cfg["tv"], precise=hp, prev=new[k]["h"])
                new[k]["h"] = (h,)
                xres[k] = mmw(k, hp, yz, b_w_out[j], res=xres[k], tm=cfg["tm"], tn=1024, name="b_out")
                tv = T if k == "p" else Ts
                raw = proj.reshape(B, T, B_PROJ_W)[:, tv - (CONV_W - 1):tv, B_INNER:2 * B_INNER + GN2]
                new[k]["bconv"].append(raw)

        kp = mmw("p", hp, mem_flat, x_wk[i], gain=norm_mem[i], tm=1024, tn=1024, name="mem_k")
        vp = mmw("p", hp, mem_flat, x_wv[i], gain=norm_mem[i], tm=1024, tn=1024, name="mem_v")
        pk.append(kp); pv.append(vp)
        for k, cfg in groups.items():
            B, T = cfg["B"], cfg["T"]
            pr = hp
            q = mmw(k, hp, xres[k], x_wq[i], gain=norm_xattn[i], tm=cfg["tm"], tn=1024,
                    out_dtype=F32 if pr else BF16, name="x_q")
            if k == "p":
                o = _xattn(q, kp.reshape(1, B, mem_len, D), vp.reshape(1, B, mem_len, D), layer=0, Bsz=B, T=T,
                           tq=cfg["tq"], precise=pr)
            else:
                o = _xattn_rows(q, mem_kv["s"][0], mem_kv["s"][1], layer=i, Bsz=B, T=T, precise=pr)
            xres[k] = mmw(k, hp, o, x_wo[i], res=xres[k], tm=cfg["tm"], tn=1024, name="x_o")

        if i % 2 == 0:
            ws = (f_w_gate[j], f_w_up[j], f_w_down[j])
            for k, cfg in groups.items():
                if hp:
                    parts = [_split_bf16(w) for w in ws]
                    xres[k] = _swiglu(xres[k], norm_ffn[i], *[p[0] for p in parts], lo=tuple(p[1] for p in parts),
                                      tm=1024, tf=256)
                else:
                    xres[k] = _swiglu(xres[k], norm_ffn[i], *[w.astype(BF16) for w in ws], tm=512, tf=1408)
        else:
            xres["p"], xres["s"] = _moe_layer(
                xres["p"], xres["s"], norm_ffn[i], _pad_cols(e_router[j], LANES),
                e_w_gate[j].astype(BF16), e_w_up[j].astype(BF16), e_w_down[j].astype(BF16))

    y_p = _final_norm(xres["p"], norm_final, 1024).reshape(Bp, Tp, D)
    y_s = _final_norm(xres["s"], norm_final, 1024).reshape(Bs, Tpad, D)[:, :Ts]

    def states(k):
        C, n, m = new[k]["Cnm"]
        h = new[k]["h"][0]
        return (C, n, m[..., 0], jnp.stack(new[k]["aconv"]),
                h.reshape(h.shape[0], h.shape[1], B_HEADS, B_HEADDIM, B_STATE), jnp.stack(new[k]["bconv"]))

    p_mem_k = jnp.stack(pk).reshape(DEPTH, Bp, mem_len, X_HEADS, X_HD)
    p_mem_v = jnp.stack(pv).reshape(DEPTH, Bp, mem_len, X_HEADS, X_HD)
    return (y_p, y_s) + states("p") + (p_mem_k, p_mem_v) + states("s")
```

```python
import functools

import jax
import jax.numpy as jnp
from jax import lax
from jax.experimental import pallas as pl
from jax.experimental.pallas import tpu as pltpu

F32 = jnp.float32
BF16 = jnp.bfloat16

D_MODEL = 1024
DEPTH = 4
CONV_W = 4
EPS = 1e-6
M_INIT = -1e30
A_HEADS = 4
A_DQK = 256
A_DV = 512
A_QK = A_HEADS * A_DQK
A_VD = A_HEADS * A_DV
B_INNER = 2 * D_MODEL
B_HEADDIM = 64
B_HEADS = B_INNER // B_HEADDIM
B_GROUPS = 4
B_HPG = B_HEADS // B_GROUPS
B_STATE = 128
B_GW = B_HPG * B_HEADDIM
X_HEADS = 4
X_HD = D_MODEL // X_HEADS
N_EXPERTS = 8
D_FF_E = 3584

LANES = 128
SUBLANES = 8
VMEM_LIMIT_BYTES = 56 * 1024 * 1024

A_PROJ_W = 2 * A_QK + 2 * A_VD + 256
B_PROJ_W = 2 * B_INNER + 2 * B_GROUPS * B_STATE + 256
A_GATE_BLK = (2 * A_QK + 2 * A_VD) // LANES
B_DT_BLK = (2 * B_INNER + 2 * B_GROUPS * B_STATE) // LANES

SAMPLE_PAD_T = 8
MOE_TM = 512
MOE_TF = 1792
SCAN_L = 256
PRECISE_LAYERS = 2
PRECISE_TM = 512


def _cparams(sem):
    return pltpu.CompilerParams(dimension_semantics=sem, vmem_limit_bytes=VMEM_LIMIT_BYTES)


def _silu(x):
    return x * jax.nn.sigmoid(x)


def _softplus(x):
    return jnp.maximum(x, 0.0) + jnp.log1p(jnp.exp(-jnp.abs(x)))


def _rms_rows(x):
    return x * lax.rsqrt(jnp.mean(x * x, axis=-1, keepdims=True) + EPS)


def _hl(x):
    hi = x.astype(BF16)
    return hi, (x - hi.astype(F32)).astype(BF16)


def _dotx(a, b, dims, precise):
    d = lambda u, v: lax.dot_general(u, v, dims, preferred_element_type=F32)
    if not precise:
        return d(a.astype(BF16), b.astype(BF16))
    ah, al = _hl(a.astype(F32))
    bh, bl = _hl(b.astype(F32))
    return d(ah, bh) + (d(al, bh) + d(ah, bl))


def _dot_w(xh, xl, w_ref, wl_ref):
    acc = jnp.dot(xh, w_ref[...], preferred_element_type=F32)
    if wl_ref is not None:
        acc = acc + (jnp.dot(xl, w_ref[...], preferred_element_type=F32)
                     + jnp.dot(xh, wl_ref[...], preferred_element_type=F32))
    return acc


def _mm_kernel(*refs, has_gain, has_res, stage, precise):
    it = iter(refs)
    x_ref = next(it)
    w_ref = next(it)
    wl_ref = next(it) if precise else None
    g_ref = next(it) if has_gain else None
    r_ref = next(it) if has_res else None
    o_ref = next(it)
    xs_ref = next(it) if stage else None
    xl_ref = next(it) if precise else None

    if stage:
        @pl.when(pl.program_id(1) == 0)
        def _():
            x = x_ref[...].astype(F32)
            if has_gain:
                x = _rms_rows(x) * g_ref[...]
            if precise:
                xs_ref[...], xl_ref[...] = _hl(x)
            else:
                xs_ref[...] = x.astype(BF16)
        lhs = xs_ref[...]
    else:
        lhs = x_ref[...]
    acc = _dot_w(lhs, xl_ref[...] if precise else None, w_ref, wl_ref)
    if has_res:
        acc = acc + r_ref[...]
    o_ref[...] = acc.astype(o_ref.dtype)


def _mm(x, w, *, w_lo=None, gain=None, res=None, tm, tn, out_dtype=F32, name="mm"):
    M, K = x.shape
    N = w.shape[1]
    tm = min(tm, M)
    assert M % tm == 0 and N % tn == 0, (M, tm, N, tn)
    precise = w_lo is not None
    stage = precise or gain is not None or x.dtype != BF16
    wspec = pl.BlockSpec((K, tn), lambda i, j: (0, j))
    ins = [x, w]
    specs = [pl.BlockSpec((tm, K), lambda i, j: (i, 0)), wspec]
    if precise:
        ins.append(w_lo)
        specs.append(wspec)
    if gain is not None:
        ins.append(gain.reshape(1, K).astype(F32))
        specs.append(pl.BlockSpec((1, K), lambda i, j: (0, 0)))
    if res is not None:
        ins.append(res)
        specs.append(pl.BlockSpec((tm, tn), lambda i, j: (i, j)))
    scratch = [pltpu.VMEM((tm, K), BF16)] * (2 if precise else 1) if stage else []
    return pl.pallas_call(
        functools.partial(_mm_kernel, has_gain=gain is not None, has_res=res is not None, stage=stage,
                          precise=precise),
        grid=(M // tm, N // tn),
        in_specs=specs,
        out_specs=pl.BlockSpec((tm, tn), lambda i, j: (i, j)),
        out_shape=jax.ShapeDtypeStruct((M, N), out_dtype),
        scratch_shapes=scratch,
        compiler_params=_cparams(("parallel", "arbitrary")),
        name=name,
    )(*ins)


def _split_bf16(w):
    bits = lax.bitcast_convert_type(w, jnp.uint32)
    hi = lax.bitcast_convert_type(bits & jnp.uint32(0xFFFF0000), F32)
    return hi.astype(BF16), (w - hi).astype(BF16)


def _norm_kernel(x_ref, g_ref, o_ref):
    o_ref[...] = _rms_rows(x_ref[...]) * g_ref[...]


def _final_norm(x, g, tm):
    M, K = x.shape
    tm = min(tm, M)
    return pl.pallas_call(
        _norm_kernel,
        grid=(M // tm,),
        in_specs=[pl.BlockSpec((tm, K), lambda i: (i, 0)), pl.BlockSpec((1, K), lambda i: (0, 0))],
        out_specs=pl.BlockSpec((tm, K), lambda i: (i, 0)),
        out_shape=jax.ShapeDtypeStruct((M, K), F32),
        compiler_params=_cparams(("parallel",)),
        name="final_norm",
    )(x, g.reshape(1, K))


def _conv_silu(x, tail_ref, w_ref, b_ref):
    L = x.shape[0]
    xp = jnp.concatenate([tail_ref[...], x], axis=0)
    y = b_ref[...] + w_ref[CONV_W - 1:CONV_W, :] * x
    for j in range(CONV_W - 1):
        shifted = pltpu.roll(xp, CONV_W - 1 - j, axis=0)[SUBLANES:SUBLANES + L]
        y = y + w_ref[j:j + 1, :] * shifted
    tail_ref[...] = x[L - SUBLANES:L]
    return _silu(y)


def _cumsum_rows(x):
    L = x.shape[0]
    row = lax.broadcasted_iota(jnp.int32, x.shape, 0)
    d = 1
    while d < L:
        x = x + jnp.where(row >= d, pltpu.roll(x, d, axis=0), 0.0)
        d *= 2
    return x


def _col_to_row(col, eye):
    return jnp.sum(jnp.where(eye, col, 0.0), axis=0, keepdims=True)


def _store_state(own_slot, out_refs, state_refs):
    for o_ref, s_ref in zip(out_refs, state_refs):
        if own_slot is None:
            o_ref[...] = s_ref[...]
        else:
            for l in range(o_ref.shape[0]):
                o_ref[l] = s_ref[...] if l == own_slot else jnp.zeros(s_ref.shape, s_ref.dtype)


def _state_specs(shapes, n_layers, layer, Bsz, prev):
    def spec(shape):
        zeros = (0,) * len(shape)
        if prev is None:
            return pl.BlockSpec((n_layers, None) + shape, lambda b, c: (0, b) + zeros)
        return pl.BlockSpec((None, None) + shape, lambda b, c: (layer, b) + zeros)
    return ([spec(s) for s in shapes],
            [jax.ShapeDtypeStruct((n_layers, Bsz) + s, F32) for s in shapes])


NN_DIMS = (((1,), (0,)), ((), ()))
NT_DIMS = (((1,), (1,)), ((), ()))
TN_DIMS = (((0,), (0,)), ((), ()))


def _mlstm_kernel(qk_ref, v_ref, o_ref, gt_ref, cw_ref, cb_ref, gb_ref, gout_ref, buf_ref,
                  C0_ref, n0_ref, m0_ref, *rest, nc, t_valid, precise, own_slot):
    hn_ref, Cout_ref, nout_ref, mout_ref, C_s, n_s, m_s, tail_s = rest[-8:]
    L = qk_ref.shape[0]
    c = pl.program_id(1)
    mm = functools.partial(_dotx, precise=precise)

    @pl.when(c == 0)
    def _():
        C_s[...] = C0_ref[...]
        n_s[...] = n0_ref[...]
        m_s[...] = m0_ref[...]
        tail_s[...] = buf_ref[...]

    qk = _conv_silu(qk_ref[...], tail_s, cw_ref, cb_ref)

    g = gt_ref[...] + gb_ref[...]
    lane = lax.broadcasted_iota(jnp.int32, g.shape, 1)
    logf = jnp.minimum(g, 0.0) - jnp.log1p(jnp.exp(-jnp.abs(g)))
    if t_valid < L:
        valid = lax.broadcasted_iota(jnp.int32, g.shape, 0) < t_valid
        logf = jnp.where(valid, logf, 0.0)
        g = jnp.where(valid, g, -jnp.inf)
    cum = _cumsum_rows(jnp.where(lane >= A_HEADS, logf, 0.0))

    ri = lax.broadcasted_iota(jnp.int32, (L, L), 0)
    ci = lax.broadcasted_iota(jnp.int32, (L, L), 1)
    eye = ri == ci
    causal = ri >= ci

    for h in range(A_HEADS):
        qf = qk[:, h * A_DQK:(h + 1) * A_DQK]
        kf = qk[:, A_QK + h * A_DQK:A_QK + (h + 1) * A_DQK] * (A_DQK ** -0.5)
        vf = v_ref[:, h * A_DV:(h + 1) * A_DV]
        igc = g[:, h:h + 1]
        cumc = cum[:, A_HEADS + h:A_HEADS + h + 1]
        m_prev = m_s[h:h + 1, 0:1]
        bcol = igc - cumc
        brow = _col_to_row(bcol, eye)
        log_intra = jnp.where(causal, cumc + brow, -jnp.inf)
        log_inter = cumc + m_prev
        m_t = jnp.maximum(log_inter, jnp.max(log_intra, axis=-1, keepdims=True))
        w_intra = jnp.exp(log_intra - m_t)
        w_inter = jnp.exp(log_inter - m_t)
        s = mm(qf, kf, NT_DIMS) * w_intra
        C_h = C_s[h]
        num = mm(s, vf, NN_DIMS) + mm(qf, C_h, NT_DIMS) * w_inter
        n_h = n_s[h:h + 1, :]
        den = jnp.sum(s, axis=-1, keepdims=True) + w_inter * jnp.sum(qf * n_h, axis=-1, keepdims=True)
        den = jnp.maximum(jnp.abs(den), jnp.exp(-m_t))
        hh = jax.nn.sigmoid(o_ref[:, h * A_DV:(h + 1) * A_DV]) * (num / den)
        hn_ref[:, h * A_DV:(h + 1) * A_DV] = (
            _rms_rows(hh) * gout_ref[:, h * A_DV:(h + 1) * A_DV]).astype(hn_ref.dtype)

        cum_l = cumc[L - 1:L, :]
        m_l = m_t[L - 1:L, :]
        w_last = jnp.exp(bcol + (cum_l - m_l))
        decay = jnp.exp(cum_l + m_prev - m_l)
        kw = w_last * kf
        upd = mm(vf, kw, TN_DIMS)
        C_s[h] = decay * C_h + upd
        n_s[h:h + 1, :] = decay * n_h + jnp.sum(kw, axis=0, keepdims=True)
        m_s[h:h + 1, :] = jnp.broadcast_to(m_l, (1, LANES))

    @pl.when(c == nc - 1)
    def _():
        _store_state(own_slot, (Cout_ref, nout_ref, mout_ref), (C_s, n_s, m_s))


def _mlstm(proj, conv_w, conv_b, gate_b, g_out, buf8, C0, n0, m0, *, layer, Bsz, T, L, t_valid, precise=False,
           prev=None):
    nc = T // L
    st_specs, st_shapes = _state_specs([(A_HEADS, A_DV, A_DQK), (A_HEADS, A_DQK), (A_HEADS, LANES)],
                                       C0.shape[0], layer, Bsz, prev)
    n_in = 12
    row = lambda b, c: b * nc + c
    per_b3 = lambda b, c: (b, 0, 0)
    lay_b5 = lambda b, c: (layer, b, 0, 0, 0)
    lay_b4 = lambda b, c: (layer, b, 0, 0)
    const2 = lambda b, c: (0, 0)
    return pl.pallas_call(
        functools.partial(_mlstm_kernel, nc=nc, t_valid=t_valid, precise=precise,
                          own_slot=layer if prev is None else None),
        grid=(Bsz, nc),
        in_specs=[
            pl.BlockSpec((L, 2 * A_QK), lambda b, c: (row(b, c), 0)),
            pl.BlockSpec((L, A_VD), lambda b, c: (row(b, c), 1)),
            pl.BlockSpec((L, A_VD), lambda b, c: (row(b, c), 2)),
            pl.BlockSpec((L, LANES), lambda b, c: (row(b, c), A_GATE_BLK)),
            pl.BlockSpec((CONV_W, 2 * A_QK), const2),
            pl.BlockSpec((1, 2 * A_QK), const2),
            pl.BlockSpec((1, LANES), const2),
            pl.BlockSpec((1, A_VD), const2),
            pl.BlockSpec((None, SUBLANES, 2 * A_QK), per_b3),
            pl.BlockSpec((None, None, A_HEADS, A_DV, A_DQK), lay_b5),
            pl.BlockSpec((None, None, A_HEADS, A_DQK), lay_b4),
            pl.BlockSpec((None, None, A_HEADS, LANES), lay_b4),
        ] + [pl.BlockSpec(memory_space=pl.ANY)] * (0 if prev is None else 3),
        out_specs=[pl.BlockSpec((L, A_VD), lambda b, c: (row(b, c), 0))] + st_specs,
        out_shape=[jax.ShapeDtypeStruct((Bsz * T, A_VD), F32 if precise else BF16)] + st_shapes,
        input_output_aliases={} if prev is None else {n_in: 1, n_in + 1: 2, n_in + 2: 3},
        scratch_shapes=[
            pltpu.VMEM((A_HEADS, A_DV, A_DQK), F32),
            pltpu.VMEM((A_HEADS, A_DQK), F32),
            pltpu.VMEM((A_HEADS, LANES), F32),
            pltpu.VMEM((SUBLANES, 2 * A_QK), F32),
        ],
        compiler_params=_cparams(("parallel", "arbitrary")),
        name="mlstm_scan",
    )(proj, proj, proj, proj, conv_w, conv_b, gate_b, g_out, buf8, C0, n0, m0, *(prev or ()))


def _ssd_kernel(z_ref, x_ref, bc_ref, dt_ref, cwx_ref, cbx_ref, cwbc_ref, cbbc_ref, dtb_ref, alog_ref,
                dskip_ref, gn_ref, bufx_ref, bufbc_ref, h0_ref, *rest, nc, t_valid, precise, own_slot):
    y_ref, hout_ref, h_s, tailx_s, tailbc_s, xw_s = rest[-6:]
    L = z_ref.shape[0]
    c = pl.program_id(1)
    mm = functools.partial(_dotx, precise=precise)

    @pl.when(c == 0)
    def _():
        h_s[...] = h0_ref[...]
        tailx_s[...] = bufx_ref[...]
        tailbc_s[...] = bufbc_ref[...]

    xs = _conv_silu(x_ref[...], tailx_s, cwx_ref, cbx_ref)
    bc = _conv_silu(bc_ref[...], tailbc_s, cwbc_ref, cbbc_ref)

    dt = _softplus(dt_ref[...] + dtb_ref[...])
    lane = lax.broadcasted_iota(jnp.int32, dt.shape, 1)
    keep = lane < B_HEADS
    if t_valid < L:
        keep = keep & (lax.broadcasted_iota(jnp.int32, dt.shape, 0) < t_valid)
    dt = jnp.where(keep, dt, 0.0)
    cum = _cumsum_rows(dt * (-jnp.exp(alog_ref[...])))
    ecum = jnp.exp(cum)
    cum_l = cum[L - 1:L, :]
    wl = jnp.exp(cum_l - cum)
    edecay = jnp.exp(cum_l)

    ri = lax.broadcasted_iota(jnp.int32, (L, L), 0)
    ci = lax.broadcasted_iota(jnp.int32, (L, L), 1)
    eye = ri == ci
    causal = ri >= ci
    lo = lax.broadcasted_iota(jnp.int32, (L, LANES), 1) < B_HEADDIM

    GN = B_GROUPS * B_STATE
    for g in range(B_GROUPS):
        Bm = bc[:, g * B_STATE:(g + 1) * B_STATE]
        Cm = bc[:, GN + g * B_STATE:GN + (g + 1) * B_STATE]
        cb = mm(Cm, Bm, NT_DIMS)
        h_g = h_s[g * B_GW:(g + 1) * B_GW, :]
        inter = mm(Cm, h_g, NT_DIMS)
        sq = jnp.zeros((L, 1), F32)
        ys = []
        for j in range(B_HPG // 2):
            col0 = g * B_GW + j * LANES
            xpair = xs[:, col0:col0 + LANES]
            hd0 = g * B_HPG + 2 * j
            xdt = xpair * jnp.where(lo, dt[:, hd0:hd0 + 1], dt[:, hd0 + 1:hd0 + 2])
            y = jnp.zeros((L, LANES), F32)
            for half in range(2):
                cumc = cum[:, hd0 + half:hd0 + half + 1]
                seg = jnp.exp(jnp.where(causal, cumc - _col_to_row(cumc, eye), -jnp.inf))
                xh = jnp.where(lo if half == 0 else ~lo, xdt, 0.0)
                y = y + mm(cb * seg, xh, NN_DIMS)
            e_pair = jnp.where(lo, ecum[:, hd0:hd0 + 1], ecum[:, hd0 + 1:hd0 + 2])
            w_pair = jnp.where(lo, wl[:, hd0:hd0 + 1], wl[:, hd0 + 1:hd0 + 2])
            y = y + e_pair * inter[:, j * LANES:(j + 1) * LANES]
            y = y + dskip_ref[:, col0:col0 + LANES] * xpair
            y = y * _silu(z_ref[:, col0:col0 + LANES])
            sq = sq + jnp.sum(y * y, axis=-1, keepdims=True)
            ys.append(y)
            xw_s[:, j * LANES:(j + 1) * LANES] = (xdt * w_pair).astype(xw_s.dtype)
        scale = lax.rsqrt(sq * (1.0 / B_GW) + EPS)
        for j in range(B_HPG // 2):
            col0 = g * B_GW + j * LANES
            y_ref[:, col0:col0 + LANES] = (ys[j] * scale * gn_ref[:, col0:col0 + LANES]).astype(y_ref.dtype)
        upd = mm(xw_s[...], Bm, TN_DIMS)
        for r in range(B_HPG):
            hd = g * B_HPG + r
            r0 = g * B_GW + r * B_HEADDIM
            h_s[r0:r0 + B_HEADDIM, :] = (edecay[:, hd:hd + 1] * h_g[r * B_HEADDIM:(r + 1) * B_HEADDIM, :]
                                         + upd[r * B_HEADDIM:(r + 1) * B_HEADDIM, :])

    @pl.when(c == nc - 1)
    def _():
        _store_state(own_slot, (hout_ref,), (h_s,))


def _ssd(proj, cwx, cbx, cwbc, cbbc, dtb, alog, dskip, gnorm, bufx8, bufbc8, h0, *, layer, Bsz, T, L, t_valid,
         precise=False, prev=None):
    nc = T // L
    st_specs, st_shapes = _state_specs([(B_INNER, B_STATE)], h0.shape[0], layer, Bsz, prev)
    n_in = 15
    GN2 = 2 * B_GROUPS * B_STATE
    row = lambda b, c: b * nc + c
    per_b3 = lambda b, c: (b, 0, 0)
    const2 = lambda b, c: (0, 0)
    return pl.pallas_call(
        functools.partial(_ssd_kernel, nc=nc, t_valid=t_valid, precise=precise,
                          own_slot=layer if prev is None else None),
        grid=(Bsz, nc),
        in_specs=[
            pl.BlockSpec((L, B_INNER), lambda b, c: (row(b, c), 0)),
            pl.BlockSpec((L, B_INNER), lambda b, c: (row(b, c), 1)),
            pl.BlockSpec((L, GN2), lambda b, c: (row(b, c), 2 * B_INNER // GN2)),
            pl.BlockSpec((L, LANES), lambda b, c: (row(b, c), B_DT_BLK)),
            pl.BlockSpec((CONV_W, B_INNER), const2),
            pl.BlockSpec((1, B_INNER), const2),
            pl.BlockSpec((CONV_W, GN2), const2),
            pl.BlockSpec((1, GN2), const2),
            pl.BlockSpec((1, LANES), const2),
            pl.BlockSpec((1, LANES), const2),
            pl.BlockSpec((1, B_INNER), const2),
            pl.BlockSpec((1, B_INNER), const2),
            pl.BlockSpec((None, SUBLANES, B_INNER), per_b3),
            pl.BlockSpec((None, SUBLANES, GN2), per_b3),
            pl.BlockSpec((None, None, B_INNER, B_STATE), lambda b, c: (layer, b, 0, 0)),
        ] + [pl.BlockSpec(memory_space=pl.ANY)] * (0 if prev is None else 1),
        out_specs=[pl.BlockSpec((L, B_INNER), lambda b, c: (row(b, c), 0))] + st_specs,
        out_shape=[jax.ShapeDtypeStruct((Bsz * T, B_INNER), F32 if precise else BF16)] + st_shapes,
        input_output_aliases={} if prev is None else {n_in: 1},
        scratch_shapes=[
            pltpu.VMEM((B_INNER, B_STATE), F32),
            pltpu.VMEM((SUBLANES, B_INNER), F32),
            pltpu.VMEM((SUBLANES, GN2), F32),
            pltpu.VMEM((L, B_GW), F32 if precise else BF16),
        ],
        compiler_params=_cparams(("parallel", "arbitrary")),
        name="ssd_scan",
    )(proj, proj, proj, proj, cwx, cbx, cwbc, cbbc, dtb, alog, dskip, gnorm, bufx8, bufbc8, h0, *(prev or ()))


def _xattn_kernel(q_ref, k_ref, v_ref, o_ref, *, precise):
    mm = functools.partial(_dotx, precise=precise)
    for h in range(X_HEADS):
        sl = slice(h * X_HD, (h + 1) * X_HD)
        s = mm(q_ref[:, sl], k_ref[:, sl], NT_DIMS) * (X_HD ** -0.5)
        e = jnp.exp(s - jnp.max(s, axis=-1, keepdims=True))
        pv = mm(e, v_ref[:, sl], NN_DIMS)
        o_ref[:, sl] = (pv / jnp.sum(e, axis=-1, keepdims=True)).astype(o_ref.dtype)


def _xattn(q, k, v, *, layer, Bsz, T, tq, precise=False):
    nq = T // tq
    mem = k.shape[2]
    return pl.pallas_call(
        functools.partial(_xattn_kernel, precise=precise),
        grid=(Bsz, nq),
        in_specs=[
            pl.BlockSpec((tq, D_MODEL), lambda b, t: (b * nq + t, 0)),
            pl.BlockSpec((None, None, mem, D_MODEL), lambda b, t: (layer, b, 0, 0)),
            pl.BlockSpec((None, None, mem, D_MODEL), lambda b, t: (layer, b, 0, 0)),
        ],
        out_specs=pl.BlockSpec((tq, D_MODEL), lambda b, t: (b * nq + t, 0)),
        out_shape=jax.ShapeDtypeStruct((Bsz * T, D_MODEL), F32 if precise else BF16),
        compiler_params=_cparams(("parallel", "arbitrary")),
        name="xattn",
    )(q, k, v)


def _xattn_rows_kernel(q_ref, k_ref, v_ref, o_ref, *, precise):
    mm = functools.partial(_dotx, precise=precise)
    T = q_ref.shape[0]
    q4 = jnp.concatenate([q_ref[:, h * X_HD:(h + 1) * X_HD] for h in range(X_HEADS)], axis=0)
    s = mm(k_ref[...], q4, NT_DIMS) * (X_HD ** -0.5)
    row_head = jnp.bitwise_and(lax.broadcasted_iota(jnp.int32, s.shape, 0), X_HEADS - 1)
    col = lax.broadcasted_iota(jnp.int32, s.shape, 1)
    col_head = sum((col >= h * T).astype(jnp.int32) for h in range(1, X_HEADS))
    s = jnp.where(row_head == col_head, s, -jnp.inf)
    e = jnp.exp(s - jnp.max(s, axis=0, keepdims=True))
    l_row = jnp.sum(e, axis=0, keepdims=True)
    pv = mm(e, v_ref[...], TN_DIMS)
    n = X_HEADS * T
    eye = lax.broadcasted_iota(jnp.int32, (n, n), 0) == lax.broadcasted_iota(jnp.int32, (n, n), 1)
    l_col = jnp.sum(jnp.where(eye, l_row, 0.0), axis=1, keepdims=True)
    o4 = pv / l_col
    for h in range(X_HEADS):
        o_ref[:, h * X_HD:(h + 1) * X_HD] = o4[h * T:(h + 1) * T, :].astype(o_ref.dtype)


def _xattn_rows(q, k, v, *, layer, Bsz, T, precise=False):
    rows = k.shape[2]
    kv_spec = pl.BlockSpec((None, None, rows, X_HD), lambda b: (layer, b, 0, 0))
    return pl.pallas_call(
        functools.partial(_xattn_rows_kernel, precise=precise),
        grid=(Bsz,),
        in_specs=[pl.BlockSpec((T, D_MODEL), lambda b: (b, 0)), kv_spec, kv_spec],
        out_specs=pl.BlockSpec((T, D_MODEL), lambda b: (b, 0)),
        out_shape=jax.ShapeDtypeStruct((Bsz * T, D_MODEL), F32 if precise else BF16),
        compiler_params=_cparams(("parallel",)),
        name="xattn_rows",
    )(q, k, v)


def _swiglu_kernel(*refs, nf, precise):
    if precise:
        x_ref, g_ref, wg_ref, wu_ref, wd_ref, wgl_ref, wul_ref, wdl_ref, o_ref, xs_ref, acc_ref, xl_ref = refs
    else:
        x_ref, g_ref, wg_ref, wu_ref, wd_ref, o_ref, xs_ref, acc_ref = refs
        wgl_ref = wul_ref = wdl_ref = xl_ref = None
    f = pl.program_id(1)

    @pl.when(f == 0)
    def _():
        xn = _rms_rows(x_ref[...]) * g_ref[...]
        if precise:
            xs_ref[...], xl_ref[...] = _hl(xn)
        else:
            xs_ref[...] = xn.astype(BF16)
        acc_ref[...] = jnp.zeros_like(acc_ref)

    xb = xs_ref[...]
    xl = xl_ref[...] if precise else None
    a = _dot_w(xb, xl, wg_ref, wgl_ref)
    u = _dot_w(xb, xl, wu_ref, wul_ref)
    hmid = _silu(a) * u
    hh, hl = _hl(hmid) if precise else (hmid.astype(BF16), None)
    acc_ref[...] += _dot_w(hh, hl, wd_ref, wdl_ref)

    @pl.when(f == nf - 1)
    def _():
        o_ref[...] = x_ref[...] + acc_ref[...]


def _swiglu(x, gain, wg, wu, wd, *, lo=None, tm, tf):
    M, K = x.shape
    F = wg.shape[1]
    tm = min(tm, M)
    nf = F // tf
    precise = lo is not None
    up = pl.BlockSpec((K, tf), lambda i, f: (0, f))
    down = pl.BlockSpec((tf, K), lambda i, f: (f, 0))
    return pl.pallas_call(
        functools.partial(_swiglu_kernel, nf=nf, precise=precise),
        grid=(M // tm, nf),
        in_specs=[pl.BlockSpec((tm, K), lambda i, f: (i, 0)), pl.BlockSpec((1, K), lambda i, f: (0, 0)),
                  up, up, down] + ([up, up, down] if precise else []),
        out_specs=pl.BlockSpec((tm, K), lambda i, f: (i, 0)),
        out_shape=jax.ShapeDtypeStruct((M, K), F32),
        scratch_shapes=[pltpu.VMEM((tm, K), BF16), pltpu.VMEM((tm, K), F32)]
        + ([pltpu.VMEM((tm, K), BF16)] if precise else []),
        compiler_params=_cparams(("parallel", "arbitrary")),
        name="swiglu",
    )(x, gain.reshape(1, K), wg, wu, wd, *(lo if precise else ()))


def _router_kernel(x_ref, g_ref, wr_ref, un_ref, idx_ref, wt_ref):
    un = _rms_rows(x_ref[...]) * g_ref[...]
    un_ref[...] = un
    logits = _dotx(un, wr_ref[...], NN_DIMS, True)
    lane = lax.broadcasted_iota(jnp.int32, logits.shape, 1)
    lg = jnp.where(lane < N_EXPERTS, logits, -jnp.inf)
    v1 = jnp.max(lg, axis=-1, keepdims=True)
    i1 = jnp.min(jnp.where(lg == v1, lane, LANES), axis=-1, keepdims=True)
    lg2 = jnp.where(lane == i1, -jnp.inf, lg)
    v2 = jnp.max(lg2, axis=-1, keepdims=True)
    i2 = jnp.min(jnp.where(lg2 == v2, lane, LANES), axis=-1, keepdims=True)
    e2 = jnp.exp(v2 - v1)
    w1 = 1.0 / (1.0 + e2)
    w2 = e2 / (1.0 + e2)
    idx_ref[...] = jnp.where(lane == 0, i1, jnp.where(lane == 1, i2, 0))
    wt_ref[...] = jnp.where(lane == 0, w1, jnp.where(lane == 1, w2, 0.0))


def _router(x, gain, wr_pad, tm):
    M, K = x.shape
    tm = min(tm, M)
    return pl.pallas_call(
        _router_kernel,
        grid=(M // tm,),
        in_specs=[
            pl.BlockSpec((tm, K), lambda i: (i, 0)),
            pl.BlockSpec((1, K), lambda i: (0, 0)),
            pl.BlockSpec((K, LANES), lambda i: (0, 0)),
        ],
        out_specs=[
            pl.BlockSpec((tm, K), lambda i: (i, 0)),
            pl.BlockSpec((tm, LANES), lambda i: (i, 0)),
            pl.BlockSpec((tm, LANES), lambda i: (i, 0)),
        ],
        out_shape=[
            jax.ShapeDtypeStruct((M, K), F32),
            jax.ShapeDtypeStruct((M, LANES), jnp.int32),
            jax.ShapeDtypeStruct((M, LANES), F32),
        ],
        compiler_params=_cparams(("parallel",)),
        name="moe_router",
    )(x, gain.reshape(1, K), wr_pad)


def _row_copy(src_hbm, row, dst_vmem, r, sem):
    return pltpu.make_async_copy(src_hbm.at[pl.ds(row, 1), :], dst_vmem.at[pl.ds(r, 1), :], sem)


def _moe_dispatch_kernel(dest_ref, x_ref, rows_in, rows_out, sem, *, row0):
    del rows_in
    i = pl.program_id(0)
    tm = x_ref.shape[0]

    def issue(r, carry):
        t = row0 + i * tm + r
        for slot in range(2):
            pltpu.make_async_copy(x_ref.at[pl.ds(r, 1), :], rows_out.at[pl.ds(dest_ref[2 * t + slot], 1), :],
                                  sem).start()
        return carry

    lax.fori_loop(0, tm, issue, 0, unroll=8)
    for slot in range(2):
        pltpu.make_async_copy(x_ref, rows_out.at[pl.ds(0, tm), :], sem).wait()


def _moe_dispatch(dest, un, rows, *, row0, tm):
    M, K = un.shape
    tm = min(tm, M)
    return pl.pallas_call(
        functools.partial(_moe_dispatch_kernel, row0=row0),
        grid_spec=pltpu.PrefetchScalarGridSpec(
            num_scalar_prefetch=1,
            grid=(M // tm,),
            in_specs=[pl.BlockSpec((tm, K), lambda i, d: (i, 0)), pl.BlockSpec(memory_space=pl.ANY)],
            out_specs=pl.BlockSpec(memory_space=pl.ANY),
            scratch_shapes=[pltpu.SemaphoreType.DMA(())],
        ),
        out_shape=jax.ShapeDtypeStruct(rows.shape, rows.dtype),
        input_output_aliases={2: 0},
        compiler_params=_cparams(("arbitrary",)),
        name="moe_dispatch",
    )(dest, un, rows)


def _moe_ffn_kernel(te_ref, nu_ref, x_ref, wg_ref, wu_ref, wd_ref, o_ref, xs_ref, acc_ref, *, nf):
    i = pl.program_id(0)
    f = pl.program_id(1)

    @pl.when(i < nu_ref[0])
    def _():
        @pl.when(f == 0)
        def _():
            xs_ref[...] = x_ref[...].astype(BF16)
            acc_ref[...] = jnp.zeros_like(acc_ref)

        xb = xs_ref[...]
        a = jnp.dot(xb, wg_ref[...], preferred_element_type=F32)
        u = jnp.dot(xb, wu_ref[...], preferred_element_type=F32)
        acc_ref[...] += jnp.dot((_silu(a) * u).astype(BF16), wd_ref[...], preferred_element_type=F32)

        @pl.when(f == nf - 1)
        def _():
            o_ref[...] = acc_ref[...]

    @pl.when((i >= nu_ref[0]) & (f == 0))
    def _():
        o_ref[...] = jnp.zeros_like(o_ref)


def _moe_ffn(tile_expert, n_used, xs, wg, wu, wd, n_tiles, layer):
    K = xs.shape[1]
    nf = D_FF_E // MOE_TF

    def tile(i, nu):
        return jnp.minimum(i, nu[0] - 1)

    def fsel(i, f, nu):
        return jnp.where(i < nu[0], f, nf - 1)

    return pl.pallas_call(
        functools.partial(_moe_ffn_kernel, nf=nf),
        grid_spec=pltpu.PrefetchScalarGridSpec(
            num_scalar_prefetch=2,
            grid=(n_tiles, nf),
            in_specs=[
                pl.BlockSpec((MOE_TM, K), lambda i, f, te, nu: (tile(i, nu), 0)),
                pl.BlockSpec((None, None, K, MOE_TF),
                             lambda i, f, te, nu: (layer, te[tile(i, nu)], 0, fsel(i, f, nu))),
                pl.BlockSpec((None, None, K, MOE_TF),
                             lambda i, f, te, nu: (layer, te[tile(i, nu)], 0, fsel(i, f, nu))),
                pl.BlockSpec((None, None, MOE_TF, K),
                             lambda i, f, te, nu: (layer, te[tile(i, nu)], fsel(i, f, nu), 0)),
            ],
            out_specs=pl.BlockSpec((MOE_TM, K), lambda i, f, te, nu: (i, 0)),
            scratch_shapes=[pltpu.VMEM((MOE_TM, K), BF16), pltpu.VMEM((MOE_TM, K), F32)],
        ),
        out_shape=jax.ShapeDtypeStruct((n_tiles * MOE_TM, K), F32),
        compiler_params=_cparams(("arbitrary", "arbitrary")),
        name="moe_ffn",
    )(tile_expert, n_used, xs, wg, wu, wd)


def _moe_combine_kernel(dest_ref, x_ref, wt_ref, ys_hbm, o_ref, buf, sem, *, row0):
    i = pl.program_id(0)
    tc = x_ref.shape[0]

    def issue(r, carry):
        t = row0 + i * tc + r
        _row_copy(ys_hbm, dest_ref[2 * t], buf.at[0], r, sem).start()
        _row_copy(ys_hbm, dest_ref[2 * t + 1], buf.at[1], r, sem).start()
        return carry

    lax.fori_loop(0, tc, issue, 0, unroll=8)
    pltpu.make_async_copy(ys_hbm.at[pl.ds(0, tc), :], buf.at[0], sem).wait()
    pltpu.make_async_copy(ys_hbm.at[pl.ds(0, tc), :], buf.at[1], sem).wait()
    wt = wt_ref[...]
    o_ref[...] = x_ref[...] + (wt[:, 0:1] * buf[0] + wt[:, 1:2] * buf[1])


def _moe_combine(dest, x, wts, ys, *, row0, tc):
    M, K = x.shape
    tc = min(tc, M)
    return pl.pallas_call(
        functools.partial(_moe_combine_kernel, row0=row0),
        grid_spec=pltpu.PrefetchScalarGridSpec(
            num_scalar_prefetch=1,
            grid=(M // tc,),
            in_specs=[
                pl.BlockSpec((tc, K), lambda i, d: (i, 0)),
                pl.BlockSpec((tc, LANES), lambda i, d: (i, 0)),
                pl.BlockSpec(memory_space=pl.ANY),
            ],
            out_specs=pl.BlockSpec((tc, K), lambda i, d: (i, 0)),
            scratch_shapes=[pltpu.VMEM((2, tc, K), F32), pltpu.SemaphoreType.DMA(())],
        ),
        out_shape=jax.ShapeDtypeStruct((M, K), F32),
        compiler_params=_cparams(("arbitrary",)),
        name="moe_combine",
    )(dest, x, wts, ys)


def _moe_plan(idx_all, n_tiles):
    N = idx_all.shape[0]
    sel = (idx_all[:, :, None] == jnp.arange(N_EXPERTS, dtype=jnp.int32)).astype(jnp.int32)
    per_tok = jnp.sum(sel, axis=1)
    before = jnp.cumsum(per_tok, axis=0) - per_tok
    counts = jnp.sum(per_tok, axis=0)
    padded = ((counts + MOE_TM - 1) // MOE_TM) * MOE_TM
    g_end = jnp.cumsum(padded)
    g_start = g_end - padded
    dest = jnp.sum(sel * (g_start + before)[:, None, :], axis=2)
    starts = jnp.arange(n_tiles, dtype=jnp.int32) * MOE_TM
    tile_expert = jnp.minimum(jnp.sum((starts[:, None] >= g_end[None, :]).astype(jnp.int32), axis=1),
                              N_EXPERTS - 1).astype(jnp.int32)
    n_used = (g_end[-1:] // MOE_TM).astype(jnp.int32)
    return dest.reshape(-1).astype(jnp.int32), tile_expert, n_used


def _moe_layer(xp, xs, gain, wr_pad, wg, wu, wd, layer):
    Np, Ns = xp.shape[0], xs.shape[0]
    un_p, idx_p, wt_p = _router(xp, gain, wr_pad, 1024)
    un_s, idx_s, wt_s = _router(xs, gain, wr_pad, 1024)
    idx_all = jnp.concatenate([idx_p[:, :2], idx_s[:, :2]], axis=0)
    n_tiles = (2 * (Np + Ns)) // MOE_TM + N_EXPERTS
    dest, tile_expert, n_used = _moe_plan(idx_all, n_tiles)
    rows = jnp.zeros((n_tiles * MOE_TM, xp.shape[1]), F32)
    rows = _moe_dispatch(dest, un_p, rows, row0=0, tm=MOE_TM)
    rows = _moe_dispatch(dest, un_s, rows, row0=Np, tm=MOE_TM)
    ys = _moe_ffn(tile_expert, n_used, rows, wg, wu, wd, n_tiles, layer)
    xp = _moe_combine(dest, xp, wt_p, ys, row0=0, tc=256)
    xs = _moe_combine(dest, xs, wt_s, ys, row0=Np, tc=256)
    return xp, xs


def _pad_cols(w, width):
    return jnp.pad(w, ((0, 0), (0, width - w.shape[1])))


def _tail8(buf):
    return jnp.pad(buf, ((0, 0), (SUBLANES - (CONV_W - 1), 0), (0, 0)))


def kernel(x_prompt, x_sample, mem_prompt, state_mlstm_C, state_mlstm_n, state_mlstm_m, state_mlstm_conv,
           state_ssd_h, state_ssd_conv, cache_mem_k, cache_mem_v, norm_mix, norm_xattn, norm_mem, norm_ffn,
           norm_final, a_w_in, a_conv_w, a_conv_b, a_b_i, a_b_f, a_norm, a_w_out, b_w_in, b_conv_w, b_conv_b,
           b_dt_bias, b_A_log, b_D, b_norm, b_w_out, x_wq, x_wk, x_wv, x_wo, f_w_gate, f_w_up, f_w_down,
           e_router, e_w_gate, e_w_up, e_w_down):
    Bp, Tp, D = x_prompt.shape
    Bs, Ts, _ = x_sample.shape
    Tpad = SAMPLE_PAD_T
    Np, Ns = Bp * Tp, Bs * Tpad
    mem_len = mem_prompt.shape[1]
    GN2 = 2 * B_GROUPS * B_STATE

    groups = {
        "p": dict(B=Bp, T=Tp, L=SCAN_L, tv=SCAN_L, tm=1024, tq=512),
        "s": dict(B=Bs, T=Tpad, L=Tpad, tv=Ts, tm=1024, tq=Tpad),
    }
    xres = {
        "p": x_prompt.reshape(Np, D),
        "s": jnp.pad(x_sample, ((0, 0), (0, Tpad - Ts), (0, 0))).reshape(Ns, D),
    }
    mem_kv = {"s": (cache_mem_k.reshape(DEPTH, Bs, mem_len * X_HEADS, X_HD),
                    cache_mem_v.reshape(DEPTH, Bs, mem_len * X_HEADS, X_HD))}
    mem_flat = mem_prompt.reshape(Bp * mem_len, D)

    st = {
        "p": dict(
            C=jnp.zeros((2, Bp, A_HEADS, A_DV, A_DQK), F32), n=jnp.zeros((2, Bp, A_HEADS, A_DQK), F32),
            m=jnp.full((2, Bp, A_HEADS), M_INIT, F32), aconv=jnp.zeros((2, Bp, CONV_W - 1, 2 * A_QK), F32),
            h=jnp.zeros((2, Bp, B_HEADS, B_HEADDIM, B_STATE), F32),
            bconv=jnp.zeros((2, Bp, CONV_W - 1, B_INNER + GN2), F32)),
        "s": dict(C=state_mlstm_C, n=state_mlstm_n, m=state_mlstm_m, aconv=state_mlstm_conv,
                  h=state_ssd_h, bconv=state_ssd_conv),
    }
    new = {k: dict(Cnm=None, h=None, aconv=[], bconv=[]) for k in groups}
    pk, pv = [], []

    def mmw(k, hp, x, w, **kw):
        if hp:
            hi, lo = _split_bf16(w)
            return _mm(x, hi, w_lo=lo, **dict(kw, tm=PRECISE_TM))
        return _mm(x, w.astype(BF16), **kw)

    moe_w = (e_w_gate.astype(BF16), e_w_up.astype(BF16), e_w_down.astype(BF16))

    for i in range(DEPTH):
        j = i // 2
        hp = i < PRECISE_LAYERS
        if i % 2 == 0:
            w_in = _pad_cols(a_w_in[j], A_PROJ_W)
            gate_b = _pad_cols(jnp.concatenate([a_b_i[j], a_b_f[j]])[None, :], LANES)
            for k, cfg in groups.items():
                B, T = cfg["B"], cfg["T"]
                proj = mmw(k, hp, xres[k], w_in, gain=norm_mix[i], tm=cfg["tm"], tn=1280, name="a_proj")
                s0 = st[k]
                hn, C, n, m = _mlstm(
                    proj, a_conv_w[j], a_conv_b[j][None, :], gate_b, a_norm[j][None, :], _tail8(s0["aconv"][j]),
                    s0["C"], s0["n"], jnp.broadcast_to(s0["m"][..., None], s0["m"].shape + (LANES,)),
                    layer=j, Bsz=B, T=T, L=cfg["L"], t_valid=cfg["tv"], precise=hp, prev=new[k]["Cnm"])
                new[k]["Cnm"] = (C, n, m)
                xres[k] = mmw(k, hp, hn, a_w_out[j], res=xres[k], tm=cfg["tm"], tn=1024, name="a_out")
                tv = T if k == "p" else Ts
                raw = proj.reshape(B, T, A_PROJ_W)[:, tv - (CONV_W - 1):tv, :2 * A_QK]
                new[k]["aconv"].append(raw)
        else:
            w_in = _pad_cols(b_w_in[j], B_PROJ_W)
            cw, cbias = b_conv_w[j], b_conv_b[j][None, :]
            dtb = _pad_cols(b_dt_bias[j][None, :], LANES)
            alog = _pad_cols(b_A_log[j][None, :], LANES)
            dskip = jnp.repeat(b_D[j], B_HEADDIM)[None, :]
            for k, cfg in groups.items():
                B, T = cfg["B"], cfg["T"]
                proj = mmw(k, hp, xres[k], w_in, gain=norm_mix[i], tm=cfg["tm"], tn=1792, name="b_proj")
                s0 = st[k]
                buf8 = _tail8(s0["bconv"][j])
                yz, h = _ssd(
                    proj, cw[:, :B_INNER], cbias[:, :B_INNER], cw[:, B_INNER:], cbias[:, B_INNER:], dtb, alog,
                    dskip, b_norm[j][None, :], buf8[..., :B_INNER], buf8[..., B_INNER:],
                    s0["h"].reshape(-1, B, B_INNER, B_STATE), layer=j, Bsz=B, T=T, L=cfg["L"],
                    t_valid=cfg["tv"], precise=hp, prev=new[k]["h"])
                new[k]["h"] = (h,)
                xres[k] = mmw(k, hp, yz, b_w_out[j], res=xres[k], tm=cfg["tm"], tn=1024, name="b_out")
                tv = T if k == "p" else Ts
                raw = proj.reshape(B, T, B_PROJ_W)[:, tv - (CONV_W - 1):tv, B_INNER:2 * B_INNER + GN2]
                new[k]["bconv"].append(raw)

        kp = mmw("p", hp, mem_flat, x_wk[i], gain=norm_mem[i], tm=1024, tn=1024, name="mem_k")
        vp = mmw("p", hp, mem_flat, x_wv[i], gain=norm_mem[i], tm=1024, tn=1024, name="mem_v")
        pk.append(kp); pv.append(vp)
        for k, cfg in groups.items():
            B, T = cfg["B"], cfg["T"]
            pr = hp
            q = mmw(k, hp, xres[k], x_wq[i], gain=norm_xattn[i], tm=cfg["tm"], tn=1024,
                    out_dtype=F32 if pr else BF16, name="x_q")
            if k == "p":
                o = _xattn(q, kp.reshape(1, B, mem_len, D), vp.reshape(1, B, mem_len, D), layer=0, Bsz=B, T=T,
                           tq=cfg["tq"], precise=pr)
            else:
                o = _xattn_rows(q, mem_kv["s"][0], mem_kv["s"][1], layer=i, Bsz=B, T=T, precise=pr)
            xres[k] = mmw(k, hp, o, x_wo[i], res=xres[k], tm=cfg["tm"], tn=1024, name="x_o")

        if i % 2 == 0:
            ws = (f_w_gate[j], f_w_up[j], f_w_down[j])
            for k, cfg in groups.items():
                if hp:
                    parts = [_split_bf16(w) for w in ws]
                    xres[k] = _swiglu(xres[k], norm_ffn[i], *[p[0] for p in parts], lo=tuple(p[1] for p in parts),
                                      tm=1024, tf=256)
                else:
                    xres[k] = _swiglu(xres[k], norm_ffn[i], *[w.astype(BF16) for w in ws], tm=512, tf=1408)
        else:
            xres["p"], xres["s"] = _moe_layer(
                xres["p"], xres["s"], norm_ffn[i], _pad_cols(e_router[j], LANES), *moe_w, layer=j)

    y_p = _final_norm(xres["p"], norm_final, 1024).reshape(Bp, Tp, D)
    y_s = _final_norm(xres["s"], norm_final, 1024).reshape(Bs, Tpad, D)[:, :Ts]

    def states(k):
        C, n, m = new[k]["Cnm"]
        h = new[k]["h"][0]
        return (C, n, m[..., 0], jnp.stack(new[k]["aconv"]),
                h.reshape(h.shape[0], h.shape[1], B_HEADS, B_HEADDIM, B_STATE), jnp.stack(new[k]["bconv"]))

    p_mem_k = jnp.stack(pk).reshape(DEPTH, Bp, mem_len, X_HEADS, X_HD)
    p_mem_v = jnp.stack(pv).reshape(DEPTH, Bp, mem_len, X_HEADS, X_HD)
    return (y_p, y_s) + states("p") + (p_mem_k, p_mem_v) + states("s")
```
